```python
import math
import jax
import jax.numpy as jnp
from jax import lax
import numpy as np

D_MODEL = 1024
BATCH = 2
SEQ = 8192
DEPTH = 2

Q_BLOCK = 128
N_MEM = 256
EPS = 1e-6

MLA_HEADS = 8
MLA_NOPE = 64
MLA_ROPE = 32
MLA_V = 64
MLA_Q_RANK = 256
MLA_KV_RANK = 256
ROPE_THETA = 10000.0

SB_HEADS = 8
SB_DIM = 64

DIFF_HEADS = 4
DIFF_DIM = 64

N_BRANCH = 3
BRANCH_WIDTH = 512

GATE_COLS = N_BRANCH * D_MODEL
MLA_COLS = MLA_Q_RANK + MLA_KV_RANK + MLA_ROPE
SB_COLS = 3 * SB_HEADS * SB_DIM
DIFF_QK = DIFF_HEADS * 2 * DIFF_DIM
DIFF_COLS = 3 * DIFF_QK
IN_COLS = GATE_COLS + MLA_COLS + SB_COLS + DIFF_COLS

X_HEADS = 4
X_DIM = 128

N_GROUPS = 4
EXPERTS_PER_GROUP = 8
N_EXPERTS = N_GROUPS * EXPERTS_PER_GROUP
TOP_K = 2
D_EXPERT = 256

kernel_name = "hybrid_mla_stickbreak_diffattn_hmoe"

F32 = jnp.float32


def rmsnorm(x, g):
    xf = x.astype(F32)
    y = xf * lax.rsqrt(jnp.mean(xf * xf, axis=-1, keepdims=True) + EPS)
    return (y * g.astype(F32)).astype(x.dtype)


def apply_rope(x, pos):
    half = x.shape[-1] // 2
    inv_freq = ROPE_THETA ** (-jnp.arange(half, dtype=F32) / half)
    ang = pos.astype(F32)[:, None] * inv_freq[None, :]
    cos = jnp.cos(ang)[None, :, None, :]
    sin = jnp.sin(ang)[None, :, None, :]
    xf = x.astype(F32)
    x1, x2 = xf[..., :half], xf[..., half:]
    out = jnp.concatenate([x1 * cos - x2 * sin, x1 * sin + x2 * cos], axis=-1)
    return out.astype(x.dtype)


def alibi_slopes(n_heads):
    return 2.0 ** (-8.0 * jnp.arange(1, n_heads + 1, dtype=F32) / n_heads)


def _to_blocks(a, nb):
    b = a.shape[0]
    return jnp.moveaxis(a.reshape(b, nb, Q_BLOCK, *a.shape[2:]), 1, 0)


def sweep_query_blocks(block_fn, q_side):
    b, s = q_side[0].shape[:2]
    nb = s // Q_BLOCK
    xs = (jnp.arange(nb, dtype=jnp.int32) * Q_BLOCK,) + tuple(_to_blocks(a, nb) for a in q_side)
    out = lax.map(lambda args: block_fn(*args), xs)
    out = jnp.moveaxis(out, 0, 1)
    return out.reshape(b, s, *out.shape[3:])


def mla_attention(c_q, c_kv, k_r, q_norm, w_uq, kv_norm, w_ukv, pos):
    b, s, _ = c_q.shape
    q = (rmsnorm(c_q, q_norm) @ w_uq).reshape(b, s, MLA_HEADS, MLA_NOPE + MLA_ROPE)
    q_nope = q[..., :MLA_NOPE]
    q_rope = apply_rope(q[..., MLA_NOPE:], pos)
    kv = (rmsnorm(c_kv, kv_norm) @ w_ukv).reshape(b, s, MLA_HEADS, MLA_NOPE + MLA_V)
    k_nope, v = kv[..., :MLA_NOPE], kv[..., MLA_NOPE:]
    k_rope = apply_rope(k_r[:, :, None, :], pos)[:, :, 0]
    scale = (MLA_NOPE + MLA_ROPE) ** -0.5

    def block(t0, qn, qr):
        sc = (jnp.einsum("bqhd,bkhd->bhqk", qn, k_nope, preferred_element_type=F32)
              + jnp.einsum("bqhr,bkr->bhqk", qr, k_rope, preferred_element_type=F32)) * scale
        qpos = t0 + jnp.arange(Q_BLOCK, dtype=jnp.int32)
        sc = jnp.where(pos[None, :] <= qpos[:, None], sc, -jnp.inf)
        p = jax.nn.softmax(sc, axis=-1).astype(v.dtype)
        return jnp.einsum("bhqk,bkhd->bqhd", p, v)

    o = sweep_query_blocks(block, (q_nope, q_rope))
    return o.reshape(b, s, MLA_HEADS * MLA_V)


def stick_breaking_attention(qkv, pos):
    b, s, _ = qkv.shape
    qkv = qkv.reshape(b, s, 3, SB_HEADS, SB_DIM)
    q, k, v = qkv[:, :, 0], qkv[:, :, 1], qkv[:, :, 2]
    scale = SB_DIM ** -0.5

    def block(t0, qb):
        z = jnp.einsum("bqhd,bkhd->bhqk", qb, k, preferred_element_type=F32) * scale
        qpos = t0 + jnp.arange(Q_BLOCK, dtype=jnp.int32)
        strict = pos[None, :] < qpos[:, None]
        log_keep = jnp.where(strict, jax.nn.log_sigmoid(-z), 0.0)
        after = lax.cumsum(log_keep, axis=3, reverse=True) - log_keep
        a = jnp.where(strict, jnp.exp(jax.nn.log_sigmoid(z) + after), 0.0)
        return jnp.einsum("bhqk,bkhd->bqhd", a.astype(v.dtype), v)

    o = sweep_query_blocks(block, (q,))
    return o.reshape(b, s, SB_HEADS * SB_DIM)


def differential_attention(qkv, lam_params, subln, lam_init, pos):
    b, s, _ = qkv.shape
    q = qkv[..., :DIFF_QK].reshape(b, s, DIFF_HEADS, 2, DIFF_DIM)
    k = qkv[..., DIFF_QK:2 * DIFF_QK].reshape(b, s, DIFF_HEADS, 2, DIFF_DIM)
    v = qkv[..., 2 * DIFF_QK:].reshape(b, s, DIFF_HEADS, 2 * DIFF_DIM)
    k1, k2 = k[:, :, :, 0], k[:, :, :, 1]
    lp = lam_params.astype(F32)
    lam = jnp.exp(jnp.sum(lp[0] * lp[1])) - jnp.exp(jnp.sum(lp[2] * lp[3])) + lam_init
    slopes = alibi_slopes(DIFF_HEADS)
    scale = DIFF_DIM ** -0.5

    def block(t0, q1, q2):
        qpos = t0 + jnp.arange(Q_BLOCK, dtype=jnp.int32)
        dist = (qpos[:, None] - pos[None, :]).astype(F32)
        causal = dist >= 0
        bias = -slopes[:, None, None] * dist

        def probs(qh, kh):
            sc = jnp.einsum("bqhd,bkhd->bhqk", qh, kh, preferred_element_type=F32) * scale + bias
            return jax.nn.softmax(jnp.where(causal, sc, -jnp.inf), axis=-1)

        p = probs(q1, k1) - lam * probs(q2, k2)
        return jnp.einsum("bhqk,bkhd->bqhd", p.astype(v.dtype), v)

    o = sweep_query_blocks(block, (q[:, :, :, 0], q[:, :, :, 1]))
    o = rmsnorm(o, subln) * (1.0 - lam_init)
    return o.reshape(b, s, DIFF_HEADS * 2 * DIFF_DIM)


def memory_cross_attention(h, mem_n, w_xq, w_xkv, w_xo):
    b, s, _ = h.shape
    q = (h @ w_xq).reshape(b, s, X_HEADS, X_DIM)
    kv = (mem_n @ w_xkv).reshape(b, mem_n.shape[1], 2, X_HEADS, X_DIM)
    k, v = kv[:, :, 0], kv[:, :, 1]
    sc = jnp.einsum("bshd,bmhd->bhsm", q, k, preferred_element_type=F32) * (X_DIM ** -0.5)
    p = jax.nn.softmax(sc, axis=-1).astype(v.dtype)
    o = jnp.einsum("bhsm,bmhd->bshd", p, v).reshape(b, s, X_HEADS * X_DIM)
    return o @ w_xo


def hierarchical_moe(h, w_grp, b_grp, w_exp, b_exp, w_gate, w_up, w_down):
    b, s, d = h.shape
    hf = h.reshape(b * s, d)
    n = hf.shape[0]
    grp_logits = jnp.matmul(hf, w_grp, preferred_element_type=F32) + b_grp.astype(F32)
    grp_prob = jax.nn.softmax(grp_logits, axis=-1)
    g_val, g_idx = lax.top_k(grp_logits, 1)
    g_sel = g_idx[:, 0]
    exp_logits = (jnp.matmul(hf, w_exp, preferred_element_type=F32) + b_exp.astype(F32)).reshape(n, N_GROUPS, EXPERTS_PER_GROUP)
    idx = jnp.broadcast_to(g_sel[:, None, None], (n, 1, EXPERTS_PER_GROUP))
    in_grp = jnp.take_along_axis(exp_logits, idx, axis=1)[:, 0]
    top_v, top_i = lax.top_k(in_grp, TOP_K)
    p_g = jnp.take_along_axis(grp_prob, g_sel[:, None], axis=1)
    top_w = jax.nn.softmax(top_v, axis=-1) * p_g
    expert_id = g_sel[:, None] * EXPERTS_PER_GROUP + top_i
    combine = jnp.sum(jax.nn.one_hot(expert_id, N_EXPERTS, dtype=F32) * top_w[..., None], axis=1)
    combine = combine.astype(hf.dtype)
    y = jnp.zeros_like(hf)
    for grp in range(N_GROUPS):
        sl = slice(grp * EXPERTS_PER_GROUP, (grp + 1) * EXPERTS_PER_GROUP)
        a = jnp.einsum("nd,edf->nef", hf, w_gate[sl])
        u = jnp.einsum("nd,edf->nef", hf, w_up[sl])
        act = jax.nn.silu(a) * u * combine[:, sl, None]
        y = y + jnp.einsum("nef,efd->nd", act, w_down[sl])
    return y.reshape(b, s, d)


def setup_inputs(seed: int = 0) -> dict:
    key = jax.random.key(seed)
    ks = iter(jax.random.split(key, 32))

    def nrm(shape, scale):
        return jax.random.normal(next(ks), shape, F32) * scale

    def gain(shape):
        return 1.0 + nrm(shape, 0.02)

    L, D = DEPTH, D_MODEL
    return {
        "x": nrm((BATCH, SEQ, D), 1.0),
        "mem": nrm((BATCH, N_MEM, D), 1.0),
        "attn_norm": gain((L, D)),
        "w_in": nrm((L, D, IN_COLS), D ** -0.5),
        "q_norm": gain((L, MLA_Q_RANK)),
        "w_uq": nrm((L, MLA_Q_RANK, MLA_HEADS * (MLA_NOPE + MLA_ROPE)), MLA_Q_RANK ** -0.5),
        "kv_norm": gain((L, MLA_KV_RANK)),
        "w_ukv": nrm((L, MLA_KV_RANK, MLA_HEADS * (MLA_NOPE + MLA_V)), MLA_KV_RANK ** -0.5),
        "diff_lambda": nrm((L, 4, DIFF_DIM), 0.1),
        "diff_subln": gain((L, 2 * DIFF_DIM)),
        "w_branch": nrm((L, N_BRANCH, BRANCH_WIDTH, D), BRANCH_WIDTH ** -0.5),
        "w_out": nrm((L, D, D), D ** -0.5),
        "cross_norm": gain((L, D)),
        "mem_norm": gain((D,)),
        "w_xq": nrm((L, D, X_HEADS * X_DIM), D ** -0.5),
        "w_xkv": nrm((L, D, 2 * X_HEADS * X_DIM), D ** -0.5),
        "w_xo": nrm((L, X_HEADS * X_DIM, D), (X_HEADS * X_DIM) ** -0.5),
        "ffn_norm": gain((L, D)),
        "w_router_grp": nrm((L, D, N_GROUPS), D ** -0.5),
        "b_router_grp": nrm((L, N_GROUPS), 0.01),
        "w_router_exp": nrm((L, D, N_EXPERTS), D ** -0.5),
        "b_router_exp": nrm((L, N_EXPERTS), 0.01),
        "w_exp_gate": nrm((L, N_EXPERTS, D, D_EXPERT), D ** -0.5),
        "w_exp_up": nrm((L, N_EXPERTS, D, D_EXPERT), D ** -0.5),
        "w_exp_down": nrm((L, N_EXPERTS, D_EXPERT, D), D_EXPERT ** -0.5),
        "final_norm": gain((D,)),
    }


def reference(x, mem, attn_norm, w_in, q_norm, w_uq, kv_norm, w_ukv, diff_lambda, diff_subln,
              w_branch, w_out, cross_norm, mem_norm, w_xq, w_xkv, w_xo, ffn_norm,
              w_router_grp, b_router_grp, w_router_exp, b_router_exp,
              w_exp_gate, w_exp_up, w_exp_down, final_norm):
    b, s, d = x.shape
    pos = jnp.arange(s, dtype=jnp.int32)
    mem_n = rmsnorm(mem, mem_norm)
    splits = [GATE_COLS, MLA_Q_RANK, MLA_KV_RANK, MLA_ROPE, SB_COLS]
    offs = list(np.cumsum(splits))
    for l in range(DEPTH):
        h = rmsnorm(x, attn_norm[l])
        proj = h @ w_in[l]
        gates, c_q, c_kv, k_r, sb_qkv, diff_qkv = jnp.split(proj, [int(o) for o in offs], axis=-1)
        gates = jax.nn.sigmoid(gates.reshape(b, s, N_BRANCH, d))
        o_a = mla_attention(c_q, c_kv, k_r, q_norm[l], w_uq[l], kv_norm[l], w_ukv[l], pos)
        o_b = stick_breaking_attention(sb_qkv, pos)
        lam_init = 0.8 - 0.6 * math.exp(-0.3 * l)
        o_c = differential_attention(diff_qkv, diff_lambda[l], diff_subln[l], lam_init, pos)
        merged = (gates[:, :, 0] * (o_a @ w_branch[l, 0])
                  + gates[:, :, 1] * (o_b @ w_branch[l, 1])
                  + gates[:, :, 2] * (o_c @ w_branch[l, 2]))
        x = x + merged @ w_out[l]
        x = x + memory_cross_attention(rmsnorm(x, cross_norm[l]), mem_n, w_xq[l], w_xkv[l], w_xo[l])
        x = x + hierarchical_moe(rmsnorm(x, ffn_norm[l]), w_router_grp[l], b_router_grp[l],
                                 w_router_exp[l], b_router_exp[l],
                                 w_exp_gate[l], w_exp_up[l], w_exp_down[l])
    return rmsnorm(x, final_norm)
```

```python
import functools
import math

import jax
import jax.numpy as jnp
from jax import lax
from jax.experimental import pallas as pl
from jax.experimental.pallas import tpu as pltpu

F32 = jnp.float32
BF16 = jnp.bfloat16
EPS = 1e-6
LANES = 128
VMEM_LIMIT_BYTES = 56 * 1024 * 1024

ROPE_THETA = 10000.0
MLA_HEADS, MLA_NOPE, MLA_ROPE, MLA_V = 8, 64, 32, 64
MLA_Q_RANK, MLA_KV_RANK = 256, 256
SB_HEADS, SB_DIM = 8, 64
DIFF_HEADS, DIFF_DIM = 4, 64
X_HEADS, X_DIM = 4, 128
N_GROUPS, EXPERTS_PER_GROUP = 4, 8
N_EXPERTS = N_GROUPS * EXPERTS_PER_GROUP
GROUP_LANE0 = N_EXPERTS


def _params(*semantics):
    return pltpu.CompilerParams(dimension_semantics=semantics, vmem_limit_bytes=VMEM_LIMIT_BYTES)


def _rms(x, g):
    return x * lax.rsqrt(jnp.mean(x * x, axis=-1, keepdims=True) + EPS) * g


def _dot(a, b):
    return jnp.dot(a, b, preferred_element_type=F32)


def _dot_nt(a, b):
    return lax.dot_general(a, b, (((1,), (1,)), ((), ())), preferred_element_type=F32)


def _norm_matmul_kernel(x_ref, g_ref, w_ref, o_ref, h_scr):
    @pl.when(pl.program_id(1) == 0)
    def _():
        h_scr[...] = _rms(x_ref[...], g_ref[...]).astype(BF16)

    o_ref[...] = _dot(h_scr[...], w_ref[...]).astype(o_ref.dtype)


def norm_matmul(x, g, w, *, tm, tn, out_dtype=BF16):
    n, k = x.shape
    m = w.shape[1]
    return pl.pallas_call(
        _norm_matmul_kernel,
        out_shape=jax.ShapeDtypeStruct((n, m), out_dtype),
        grid=(n // tm, m // tn),
        in_specs=[
            pl.BlockSpec((tm, k), lambda i, j: (i, 0)),
            pl.BlockSpec((1, k), lambda i, j: (0, 0)),
            pl.BlockSpec((k, tn), lambda i, j: (0, j)),
        ],
        out_specs=pl.BlockSpec((tm, tn), lambda i, j: (i, j)),
        scratch_shapes=[pltpu.VMEM((tm, k), BF16)],
        compiler_params=_params("parallel", "arbitrary"),
        name="norm_matmul",
    )(x, g.reshape(1, k), w)


def _mla_prep_kernel(x_ref, g_ref, w1_ref, qn_ref, kvn_ref, wq_ref, wqs_ref, wk_ref, wv_ref,
                     cos_ref, sin_ref, q_ref, k_ref, v_ref, *, scale):
    h = _rms(x_ref[...], g_ref[...]).astype(BF16)
    c = _dot(h, w1_ref[...])
    cq = _rms(c[:, :MLA_Q_RANK], qn_ref[...]).astype(BF16)
    ckv = _rms(c[:, MLA_Q_RANK:MLA_Q_RANK + MLA_KV_RANK], kvn_ref[...]).astype(BF16)
    base = MLA_Q_RANK + MLA_KV_RANK
    cos = cos_ref[...]
    sin = sin_ref[...]
    kr = c[:, base:base + LANES] * cos + c[:, base + LANES:base + 2 * LANES] * sin
    q = _dot(cq, wq_ref[...])
    qs = _dot(cq, wqs_ref[...])
    k = _dot(ckv, wk_ref[...])
    v_ref[...] = _dot(ckv, wv_ref[...]).astype(BF16)
    for hh in range(MLA_HEADS):
        sl = slice(hh * LANES, (hh + 1) * LANES)
        q_ref[:, sl] = ((q[:, sl] * cos + qs[:, sl] * sin) * scale).astype(BF16)
        k_ref[:, sl] = (k[:, sl] + kr).astype(BF16)


def mla_prep(x, g, w1, qn, kvn, wq, wqs, wk, wv, cos_t, sin_t, *, tm, seq):
    n, d = x.shape
    width = MLA_HEADS * LANES
    steps_per_seq = seq // tm
    full = lambda a: pl.BlockSpec(a.shape, lambda i: (0,) * a.ndim)
    out = jax.ShapeDtypeStruct((n, width), BF16)
    row = pl.BlockSpec((tm, width), lambda i: (i, 0))
    args = (g.reshape(1, d), w1, qn.reshape(1, -1), kvn.reshape(1, -1), wq, wqs, wk, wv)
    return pl.pallas_call(
        functools.partial(_mla_prep_kernel, scale=(MLA_NOPE + MLA_ROPE) ** -0.5),
        out_shape=(out, out, out),
        grid=(n // tm,),
        in_specs=[pl.BlockSpec((tm, d), lambda i: (i, 0))] + [full(a) for a in args] + [
            pl.BlockSpec((tm, LANES), lambda i: (i % steps_per_seq, 0)),
            pl.BlockSpec((tm, LANES), lambda i: (i % steps_per_seq, 0)),
        ],
        out_specs=(row, row, row),
        compiler_params=_params("parallel"),
        name="mla_prep",
    )(x, *args, cos_t, sin_t)


def _flash_kernel(q_ref, k_ref, v_ref, o_ref, m_scr, l_scr, acc_scr, *, t, q_scale, slopes, heads_per_slope):
    qi = pl.program_id(2)
    q = q_ref[...]
    if q_scale != 1.0:
        q = q * q_scale
    m_scr[...] = jnp.full(m_scr.shape, -jnp.inf, F32)
    l_scr[...] = jnp.zeros(l_scr.shape, F32)
    acc_scr[...] = jnp.zeros(acc_scr.shape, F32)

    if slopes is not None:
        slope_idx = pl.program_id(1) // heads_per_slope
        slope = jnp.float32(slopes[-1])
        for i in range(len(slopes) - 2, -1, -1):
            slope = jnp.where(slope_idx == i, jnp.float32(slopes[i]), slope)
        col = lax.broadcasted_iota(jnp.int32, (1, t), 1)

    def step(kb, diagonal):
        k = k_ref[pl.ds(pl.multiple_of(kb * t, t), t), :]
        v = v_ref[pl.ds(pl.multiple_of(kb * t, t), t), :]
        s = _dot_nt(q, k)
        if slopes is not None:
            s = s + slope * (col + (kb - qi) * t).astype(F32)
        if diagonal:
            r = lax.broadcasted_iota(jnp.int32, (t, t), 0)
            c = lax.broadcasted_iota(jnp.int32, (t, t), 1)
            s = jnp.where(c <= r, s, -jnp.inf)
        m_prev = m_scr[...]
        m_new = jnp.maximum(m_prev, jnp.max(s, axis=1, keepdims=True))
        alpha = jnp.exp(m_prev - m_new)
        p = jnp.exp(s - m_new)
        l_scr[...] = alpha * l_scr[...] + jnp.sum(p, axis=1, keepdims=True)
        acc_scr[...] = alpha * acc_scr[...] + _dot(p.astype(BF16), v)
        m_scr[...] = m_new

    def body(kb, carry):
        step(kb, False)
        return carry

    lax.fori_loop(0, qi, body, 0)
    step(qi, True)
    o_ref[...] = (acc_scr[...] / l_scr[...]).astype(o_ref.dtype)


def flash_attention(q, k, v, *, batch, seq, heads, q_col0, k_col0, v_col0, heads_per_kv, t,
                    q_scale=1.0, slopes=None, heads_per_slope=1, out_dtype=BF16):
    nq = seq // t
    kernel = functools.partial(_flash_kernel, t=t, q_scale=q_scale, slopes=slopes,
                               heads_per_slope=heads_per_slope)
    return pl.pallas_call(
        kernel,
        out_shape=jax.ShapeDtypeStruct((batch * seq, heads * LANES), out_dtype),
        grid=(batch, heads, nq),
        in_specs=[
            pl.BlockSpec((t, LANES), lambda b, h, i: (b * nq + i, q_col0 + h)),
            pl.BlockSpec((seq, LANES), lambda b, h, i: (b, k_col0 + h // heads_per_kv)),
            pl.BlockSpec((seq, LANES), lambda b, h, i: (b, v_col0 + h // heads_per_kv)),
        ],
        out_specs=pl.BlockSpec((t, LANES), lambda b, h, i: (b * nq + i, h)),
        scratch_shapes=[pltpu.VMEM((t, 1), F32), pltpu.VMEM((t, 1), F32), pltpu.VMEM((t, LANES), F32)],
        compiler_params=_params("parallel", "parallel", "arbitrary"),
        name="flash_attention",
    )(q, k, v)


def _sb_kernel(q_ref, k_ref, v_ref, tri_ref, o_ref, c_scr, acc_scr, *, tq, tk, q_scale):
    qi = pl.program_id(2)
    q = q_ref[...] * q_scale
    tri = tri_ref[...]
    c_scr[...] = jnp.zeros(c_scr.shape, F32)
    acc_scr[...] = jnp.zeros(acc_scr.shape, F32)

    def step(kc, local):
        k = k_ref[pl.ds(pl.multiple_of(kc * tk, tk), tk), :]
        v = v_ref[pl.ds(pl.multiple_of(kc * tk, tk), tk), :]
        z = _dot_nt(q, k)
        ls = jnp.minimum(z, 0.0) - jnp.log(1.0 + jnp.exp(-jnp.abs(z)))
        lk = ls - z
        if local is not None:
            r = lax.broadcasted_iota(jnp.int32, (tq, tk), 0)
            c = lax.broadcasted_iota(jnp.int32, (tq, tk), 1)
            valid = (c + local) < r
            lk = jnp.where(valid, lk, 0.0)
        lkb = lk.astype(BF16)
        after = _dot(lkb, tri)
        c_prev = c_scr[...]
        p = jnp.exp(ls + after + c_prev)
        if local is not None:
            p = jnp.where(valid, p, 0.0)
        acc_scr[...] += _dot(p.astype(BF16), v)
        c_scr[...] = c_prev + after[:, 0:1] + lkb[:, 0:1].astype(F32)

    ratio = tq // tk
    for j in range(ratio - 1, -1, -1):
        step(qi * ratio + j, j * tk)

    def body(i, carry):
        step(qi * ratio - 1 - i, None)
        return carry

    lax.fori_loop(0, qi * ratio, body, 0)
    o_ref[...] = acc_scr[...].astype(o_ref.dtype)


def sb_attention(q, k, v, *, batch, seq, heads, q_col0, k_col0, v_col0, tq, tk, q_scale):
    nq = seq // tq
    r = lax.broadcasted_iota(jnp.int32, (tk, tk), 0)
    c = lax.broadcasted_iota(jnp.int32, (tk, tk), 1)
    tri = (r > c).astype(BF16)
    return pl.pallas_call(
        functools.partial(_sb_kernel, tq=tq, tk=tk, q_scale=q_scale),
        out_shape=jax.ShapeDtypeStruct((batch * seq, heads * LANES), BF16),
        grid=(batch, heads, nq),
        in_specs=[
            pl.BlockSpec((tq, LANES), lambda b, h, i: (b * nq + i, q_col0 + h)),
            pl.BlockSpec((seq, LANES), lambda b, h, i: (b, k_col0 + h)),
            pl.BlockSpec((seq, LANES), lambda b, h, i: (b, v_col0 + h)),
            pl.BlockSpec((tk, tk), lambda b, h, i: (0, 0)),
        ],
        out_specs=pl.BlockSpec((tq, LANES), lambda b, h, i: (b * nq + i, h)),
        scratch_shapes=[pltpu.VMEM((tq, 1), F32), pltpu.VMEM((tq, LANES), F32)],
        compiler_params=_params("parallel", "parallel", "arbitrary"),
        name="sb_attention",
    )(q, k, v, tri)


def _merge_kernel(x_ref, g_ref, wg_ref, oa_ref, ob_ref, oc_ref, lam_ref, subln_ref,
                  wba_ref, wbb_ref, wbc_ref, wo_ref, o_ref, *, lam_init):
    x = x_ref[...]
    d = x.shape[1]
    h = _rms(x, g_ref[...]).astype(BF16)
    lp = lam_ref[...]
    lam = (jnp.exp(jnp.sum(lp[0:1] * lp[1:2], axis=1, keepdims=True))
           - jnp.exp(jnp.sum(lp[2:3] * lp[3:4], axis=1, keepdims=True)) + lam_init)
    oc = oc_ref[...]
    parts = []
    for hh in range(DIFF_HEADS):
        p1 = oc[:, (2 * hh) * LANES:(2 * hh + 1) * LANES]
        p2 = oc[:, (2 * hh + 1) * LANES:(2 * hh + 2) * LANES]
        parts.append((_rms(p1 - lam * p2, subln_ref[...]) * (1.0 - lam_init)).astype(BF16))
    ocn = jnp.concatenate(parts, axis=1)

    def gate(i):
        return jax.nn.sigmoid(_dot(h, wg_ref[:, i * d:(i + 1) * d]))

    merged = gate(0) * _dot(oa_ref[...], wba_ref[...])
    merged += gate(1) * _dot(ob_ref[...], wbb_ref[...])
    merged += gate(2) * _dot(ocn, wbc_ref[...])
    o_ref[...] = x + _dot(merged.astype(BF16), wo_ref[...])


def merge(x, g, wg, oa, ob, oc, lam_p, subln, wba, wbb, wbc, wo, *, lam_init, tm):
    n, d = x.shape
    full = lambda a: pl.BlockSpec(a.shape, lambda i: (0,) * a.ndim)
    row = lambda a: pl.BlockSpec((tm, a.shape[1]), lambda i: (i, 0))
    g2 = g.reshape(1, d)
    subln2 = subln.reshape(1, -1)
    return pl.pallas_call(
        functools.partial(_merge_kernel, lam_init=lam_init),
        out_shape=jax.ShapeDtypeStruct((n, d), F32),
        grid=(n // tm,),
        in_specs=[row(x), full(g2), full(wg), row(oa), row(ob), row(oc), full(lam_p), full(subln2),
                  full(wba), full(wbb), full(wbc), full(wo)],
        out_specs=row(x),
        compiler_params=_params("parallel"),
        name="merge",
    )(x, g2, wg, oa, ob, oc, lam_p, subln2, wba, wbb, wbc, wo)


def _cross_kernel(x_ref, g_ref, wq_ref, k_ref, v_ref, wo_ref, o_ref, *, scale):
    x = x_ref[...]
    h = _rms(x, g_ref[...]).astype(BF16)
    q = (_dot(h, wq_ref[...]) * scale).astype(BF16)
    outs = []
    for hh in range(X_HEADS):
        sl = slice(hh * X_DIM, (hh + 1) * X_DIM)
        s = _dot_nt(q[:, sl], k_ref[:, sl])
        p = jnp.exp(s - jnp.max(s, axis=1, keepdims=True))
        o = _dot(p.astype(BF16), v_ref[:, sl]) / jnp.sum(p, axis=1, keepdims=True)
        outs.append(o.astype(BF16))
    o_ref[...] = x + _dot(jnp.concatenate(outs, axis=1), wo_ref[...])


def cross_attention(x, g, wq, kv, wo, *, batch, seq, tm):
    n, d = x.shape
    n_mem = kv.shape[0] // batch
    width = X_HEADS * X_DIM
    steps = seq // tm
    full = lambda a: pl.BlockSpec(a.shape, lambda b, i: (0,) * a.ndim)
    g2 = g.reshape(1, d)
    return pl.pallas_call(
        functools.partial(_cross_kernel, scale=X_DIM ** -0.5),
        out_shape=jax.ShapeDtypeStruct((n, d), F32),
        grid=(batch, steps),
        in_specs=[
            pl.BlockSpec((tm, d), lambda b, i: (b * steps + i, 0)),
            full(g2), full(wq),
            pl.BlockSpec((n_mem, width), lambda b, i: (b, 0)),
            pl.BlockSpec((n_mem, width), lambda b, i: (b, 1)),
            full(wo),
        ],
        out_specs=pl.BlockSpec((tm, d), lambda b, i: (b * steps + i, 0)),
        compiler_params=_params("parallel", "parallel"),
        name="cross_attention",
    )(x, g2, wq, kv, kv, wo)


def _router_kernel(x_ref, g_ref, w_ref, b_ref, comb_ref):
    h = _rms(x_ref[...], g_ref[...])
    logits = jnp.dot(h, w_ref[...], preferred_element_type=F32, precision=lax.Precision.HIGHEST) + b_ref[...]
    lane = lax.broadcasted_iota(jnp.int32, logits.shape, 1).astype(F32)
    neg = -jnp.inf
    far = float(4 * LANES)

    def first_argmax(vals, vmax):
        return jnp.min(jnp.where(vals == vmax, lane, far), axis=1, keepdims=True)

    gl = jnp.where((lane >= GROUP_LANE0) & (lane < GROUP_LANE0 + N_GROUPS), logits, neg)
    gmax = jnp.max(gl, axis=1, keepdims=True)
    lo = (first_argmax(gl, gmax) - GROUP_LANE0) * EXPERTS_PER_GROUP
    p_g = 1.0 / jnp.sum(jnp.exp(gl - gmax), axis=1, keepdims=True)
    el = jnp.where((lane >= lo) & (lane < lo + EXPERTS_PER_GROUP), logits, neg)
    v1 = jnp.max(el, axis=1, keepdims=True)
    i1 = first_argmax(el, v1)
    el2 = jnp.where(lane == i1, neg, el)
    v2 = jnp.max(el2, axis=1, keepdims=True)
    i2 = first_argmax(el2, v2)
    t = jnp.exp(v2 - v1)
    w1 = 1.0 / (1.0 + t)
    comb_ref[...] = jnp.where(lane == i1, w1 * p_g, 0.0) + jnp.where(lane == i2, (t * w1) * p_g, 0.0)


def router(x, g, w, b, *, tm):
    n, d = x.shape
    full = lambda a: pl.BlockSpec(a.shape, lambda i: (0,) * a.ndim)
    g2 = g.reshape(1, d)
    return pl.pallas_call(
        _router_kernel,
        out_shape=jax.ShapeDtypeStruct((n, LANES), F32),
        grid=(n // tm,),
        in_specs=[pl.BlockSpec((tm, d), lambda i: (i, 0)), full(g2), full(w), full(b)],
        out_specs=pl.BlockSpec((tm, LANES), lambda i: (i, 0)),
        compiler_params=_params("parallel"),
        name="router",
    )(x, g2, w, b)


def _moe_kernel(x_ref, g_ref, comb_ref, wg_ref, wu_ref, wd_ref, fg_ref, o_ref, h_scr, y_scr, *, ec, final_norm):
    j = pl.program_id(1)

    @pl.when(j == 0)
    def _():
        h_scr[...] = _rms(x_ref[...], g_ref[...]).astype(BF16)
        y_scr[...] = jnp.zeros(y_scr.shape, F32)

    h = h_scr[...]
    comb = comb_ref[...]
    lane = lax.broadcasted_iota(jnp.int32, comb.shape, 1)
    for e in range(ec):
        a = _dot(h, wg_ref[e])
        u = _dot(h, wu_ref[e])
        cw = jnp.sum(jnp.where(lane == j * ec + e, comb, 0.0), axis=1, keepdims=True)
        act = (a * jax.nn.sigmoid(a)) * u * cw
        y_scr[...] += _dot(act.astype(BF16), wd_ref[e])

    @pl.when(j == pl.num_programs(1) - 1)
    def _():
        y = x_ref[...] + y_scr[...]
        if final_norm:
            y = _rms(y, fg_ref[...])
        o_ref[...] = y


def moe(x, g, comb, wg, wu, wd, fg, *, tm, ec, final_norm):
    n, d = x.shape
    n_exp, _, f = wg.shape
    g2 = g.reshape(1, d)
    fg2 = fg.reshape(1, d)
    return pl.pallas_call(
        functools.partial(_moe_kernel, ec=ec, final_norm=final_norm),
        out_shape=jax.ShapeDtypeStruct((n, d), F32),
        grid=(n // tm, n_exp // ec),
        in_specs=[
            pl.BlockSpec((tm, d), lambda i, j: (i, 0)),
            pl.BlockSpec((1, d), lambda i, j: (0, 0)),
            pl.BlockSpec((tm, LANES), lambda i, j: (i, 0)),
            pl.BlockSpec((ec, d, f), lambda i, j: (j, 0, 0)),
            pl.BlockSpec((ec, d, f), lambda i, j: (j, 0, 0)),
            pl.BlockSpec((ec, f, d), lambda i, j: (j, 0, 0)),
            pl.BlockSpec((1, d), lambda i, j: (0, 0)),
        ],
        out_specs=pl.BlockSpec((tm, d), lambda i, j: (i, 0)),
        scratch_shapes=[pltpu.VMEM((tm, d), BF16), pltpu.VMEM((tm, d), F32)],
        compiler_params=_params("parallel", "arbitrary"),
        name="moe",
    )(x, g2, comb, wg, wu, wd, fg2)


def _pad_heads(w, heads, dim, lane0=0):
    k = w.shape[0]
    w = w.reshape(k, heads, dim)
    w = jnp.pad(w, ((0, 0), (0, 0), (lane0, LANES - dim - lane0)))
    return w.reshape(k, heads * LANES)


def _pad_head_rows(w, heads, dim):
    m = w.shape[1]
    w = w.reshape(heads, dim, m)
    w = jnp.pad(w, ((0, 0), (0, LANES - dim), (0, 0)))
    return w.reshape(heads * LANES, m)


def _swap_halves(w):
    half = w.shape[-1] // 2
    return jnp.concatenate([w[..., half:], w[..., :half]], axis=-1)


def _rope_tables(seq):
    half = MLA_ROPE // 2
    inv_freq = ROPE_THETA ** (-jnp.arange(half, dtype=F32) / half)
    ang = jnp.arange(seq, dtype=F32)[:, None] * inv_freq[None, :]
    cos, sin = jnp.cos(ang), jnp.sin(ang)
    ones = jnp.ones((seq, MLA_NOPE), F32)
    zn = jnp.zeros((seq, MLA_NOPE), F32)
    zp = jnp.zeros((seq, LANES - MLA_NOPE - MLA_ROPE), F32)
    return (jnp.concatenate([ones, cos, cos, zp], axis=1),
            jnp.concatenate([zn, -sin, sin, zp], axis=1))


def kernel(x, mem, attn_norm, w_in, q_norm, w_uq, kv_norm, w_ukv, diff_lambda, diff_subln, w_branch, w_out,
           cross_norm, mem_norm, w_xq, w_xkv, w_xo, ffn_norm, w_router_grp, b_router_grp, w_router_exp,
           b_router_exp, w_exp_gate, w_exp_up, w_exp_down, final_norm):
    batch, seq, d = x.shape
    depth = w_in.shape[0]
    n = batch * seq
    n_mem = mem.shape[1]
    tm = min(512, seq)
    t_attn = min(512, seq)
    tk_sb = min(256, seq)

    gate_cols = 3 * d
    o_cq = gate_cols
    o_ckv = o_cq + MLA_Q_RANK
    o_kr = o_ckv + MLA_KV_RANK
    o_sb = o_kr + MLA_ROPE
    sb_w = SB_HEADS * SB_DIM
    o_diff = o_sb + 3 * sb_w
    diff_qk = DIFF_HEADS * 2 * DIFF_DIM

    cos_t, sin_t = _rope_tables(seq)
    xf = x.reshape(n, d)
    mem_f = mem.reshape(batch * n_mem, d)

    for l in range(depth):
        wl = w_in[l]
        kr_w = wl[:, o_kr:o_sb]
        w1 = jnp.concatenate([
            wl[:, o_cq:o_kr],
            jnp.pad(kr_w, ((0, 0), (MLA_NOPE, LANES - MLA_NOPE - MLA_ROPE))),
            jnp.pad(_swap_halves(kr_w), ((0, 0), (MLA_NOPE, LANES - MLA_NOPE - MLA_ROPE))),
        ], axis=1).astype(BF16)
        uq = w_uq[l].reshape(MLA_Q_RANK, MLA_HEADS, MLA_NOPE + MLA_ROPE)
        uq_nope, uq_rope = uq[..., :MLA_NOPE], uq[..., MLA_NOPE:]
        padq = ((0, 0), (0, 0), (0, LANES - MLA_NOPE - MLA_ROPE))
        wq = jnp.pad(jnp.concatenate([uq_nope, uq_rope], -1), padq).reshape(MLA_Q_RANK, -1).astype(BF16)
        wqs = jnp.pad(jnp.concatenate([jnp.zeros_like(uq_nope), _swap_halves(uq_rope)], -1), padq)
        wqs = wqs.reshape(MLA_Q_RANK, -1).astype(BF16)
        ukv = w_ukv[l].reshape(MLA_KV_RANK, MLA_HEADS, MLA_NOPE + MLA_V)
        wk = _pad_heads(ukv[..., :MLA_NOPE].reshape(MLA_KV_RANK, -1), MLA_HEADS, MLA_NOPE).astype(BF16)
        wv = _pad_heads(ukv[..., MLA_NOPE:].reshape(MLA_KV_RANK, -1), MLA_HEADS, MLA_V).astype(BF16)
        q_a, k_a, v_a = mla_prep(xf, attn_norm[l], w1, q_norm[l], kv_norm[l], wq, wqs, wk, wv,
                                 cos_t, sin_t, tm=tm, seq=seq)

        sb = wl[:, o_sb:o_diff]
        dq = wl[:, o_diff:o_diff + diff_qk].reshape(d, DIFF_HEADS, 2, DIFF_DIM)
        dq = jnp.stack([jnp.pad(dq[:, :, 0], ((0, 0), (0, 0), (0, DIFF_DIM))),
                        jnp.pad(dq[:, :, 1], ((0, 0), (0, 0), (DIFF_DIM, 0)))], axis=2)
        w2 = jnp.concatenate([
            _pad_heads(sb[:, :sb_w], SB_HEADS, SB_DIM),
            _pad_heads(sb[:, sb_w:2 * sb_w], SB_HEADS, SB_DIM),
            _pad_heads(sb[:, 2 * sb_w:], SB_HEADS, SB_DIM),
            dq.reshape(d, 2 * DIFF_HEADS * LANES),
            wl[:, o_diff + diff_qk:o_diff + 3 * diff_qk],
        ], axis=1).astype(BF16)
        p2 = norm_matmul(xf, attn_norm[l], w2, tm=tm, tn=512)
        cb_sb_q, cb_sb_k, cb_sb_v = 0, SB_HEADS, 2 * SB_HEADS
        cb_d_q = 3 * SB_HEADS
        cb_d_k = cb_d_q + 2 * DIFF_HEADS
        cb_d_v = cb_d_k + DIFF_HEADS

        o_a = flash_attention(q_a, k_a, v_a, batch=batch, seq=seq, heads=MLA_HEADS, q_col0=0, k_col0=0,
                              v_col0=0, heads_per_kv=1, t=t_attn)
        o_b = sb_attention(p2, p2, p2, batch=batch, seq=seq, heads=SB_HEADS, q_col0=cb_sb_q,
                           k_col0=cb_sb_k, v_col0=cb_sb_v, tq=t_attn, tk=tk_sb, q_scale=SB_DIM ** -0.5)
        slopes = tuple(2.0 ** (-8.0 * (i + 1) / DIFF_HEADS) for i in range(DIFF_HEADS))
        o_c = flash_attention(p2, p2, p2, batch=batch, seq=seq, heads=2 * DIFF_HEADS, q_col0=cb_d_q,
                              k_col0=cb_d_k, v_col0=cb_d_v, heads_per_kv=2, t=t_attn,
                              q_scale=DIFF_DIM ** -0.5, slopes=slopes, heads_per_slope=2, out_dtype=F32)

        lam_init = 0.8 - 0.6 * math.exp(-0.3 * l)
        wba = _pad_head_rows(w_branch[l, 0], MLA_HEADS, MLA_V).astype(BF16)
        wbb = _pad_head_rows(w_branch[l, 1], SB_HEADS, SB_DIM).astype(BF16)
        xf = merge(xf, attn_norm[l], wl[:, :gate_cols].astype(BF16), o_a, o_b, o_c, diff_lambda[l],
                   diff_subln[l], wba, wbb, w_branch[l, 2].astype(BF16), w_out[l].astype(BF16),
                   lam_init=lam_init, tm=min(256, seq))

        kv = norm_matmul(mem_f, mem_norm, w_xkv[l].astype(BF16), tm=min(256, n_mem), tn=512)
        xf = cross_attention(xf, cross_norm[l], w_xq[l].astype(BF16), kv, w_xo[l].astype(BF16),
                             batch=batch, seq=seq, tm=tm)

        pad_r = LANES - N_EXPERTS - N_GROUPS
        wr = jnp.pad(jnp.concatenate([w_router_exp[l], w_router_grp[l]], axis=1), ((0, 0), (0, pad_r)))
        br = jnp.pad(jnp.concatenate([b_router_exp[l], b_router_grp[l]]), (0, pad_r)).reshape(1, LANES)
        comb = router(xf, ffn_norm[l], wr, br, tm=tm)
        xf = moe(xf, ffn_norm[l], comb, w_exp_gate[l].astype(BF16), w_exp_up[l].astype(BF16),
                 w_exp_down[l].astype(BF16), final_norm, tm=tm, ec=4, final_norm=(l == depth - 1))

    return xf.reshape(batch, seq, d)
```

```python
import functools
import math

import jax
import jax.numpy as jnp
from jax import lax
from jax.experimental import pallas as pl
from jax.experimental.pallas import tpu as pltpu

F32 = jnp.float32
BF16 = jnp.bfloat16
EPS = 1e-6
LANES = 128
VMEM_LIMIT_BYTES = 56 * 1024 * 1024

ROPE_THETA = 10000.0
MLA_HEADS, MLA_NOPE, MLA_ROPE, MLA_V = 8, 64, 32, 64
MLA_Q_RANK, MLA_KV_RANK = 256, 256
SB_HEADS, SB_DIM = 8, 64
DIFF_HEADS, DIFF_DIM = 4, 64
X_HEADS, X_DIM = 4, 128
N_GROUPS, EXPERTS_PER_GROUP = 4, 8
N_EXPERTS = N_GROUPS * EXPERTS_PER_GROUP
GROUP_LANE0 = N_EXPERTS


def _params(*semantics):
    return pltpu.CompilerParams(dimension_semantics=semantics, vmem_limit_bytes=VMEM_LIMIT_BYTES)


def _rms(x, g):
    return x * lax.rsqrt(jnp.mean(x * x, axis=-1, keepdims=True) + EPS) * g


def _dot(a, b):
    return jnp.dot(a, b, preferred_element_type=F32)


def _dot_nt(a, b):
    return lax.dot_general(a, b, (((1,), (1,)), ((), ())), preferred_element_type=F32)


def _norm_matmul_kernel(x_ref, g_ref, w_ref, o_ref, h_scr):
    @pl.when(pl.program_id(1) == 0)
    def _():
        h_scr[...] = _rms(x_ref[...], g_ref[...]).astype(BF16)

    o_ref[...] = _dot(h_scr[...], w_ref[...]).astype(o_ref.dtype)


def norm_matmul(x, g, w, *, tm, tn, out_dtype=BF16):
    n, k = x.shape
    m = w.shape[1]
    return pl.pallas_call(
        _norm_matmul_kernel,
        out_shape=jax.ShapeDtypeStruct((n, m), out_dtype),
        grid=(n // tm, m // tn),
        in_specs=[
            pl.BlockSpec((tm, k), lambda i, j: (i, 0)),
            pl.BlockSpec((1, k), lambda i, j: (0, 0)),
            pl.BlockSpec((k, tn), lambda i, j: (0, j)),
        ],
        out_specs=pl.BlockSpec((tm, tn), lambda i, j: (i, j)),
        scratch_shapes=[pltpu.VMEM((tm, k), BF16)],
        compiler_params=_params("parallel", "arbitrary"),
        name="norm_matmul",
    )(x, g.reshape(1, k), w)


def _mla_prep_kernel(x_ref, g_ref, w1_ref, qn_ref, kvn_ref, wq_ref, wqs_ref, wk_ref, wv_ref,
                     cos_ref, sin_ref, q_ref, k_ref, v_ref, *, scale):
    h = _rms(x_ref[...], g_ref[...]).astype(BF16)
    c = _dot(h, w1_ref[...])
    cq = _rms(c[:, :MLA_Q_RANK], qn_ref[...]).astype(BF16)
    ckv = _rms(c[:, MLA_Q_RANK:MLA_Q_RANK + MLA_KV_RANK], kvn_ref[...]).astype(BF16)
    base = MLA_Q_RANK + MLA_KV_RANK
    cos = cos_ref[...]
    sin = sin_ref[...]
    kr = c[:, base:base + LANES] * cos + c[:, base + LANES:base + 2 * LANES] * sin
    q = _dot(cq, wq_ref[...])
    qs = _dot(cq, wqs_ref[...])
    k = _dot(ckv, wk_ref[...])
    v_ref[...] = _dot(ckv, wv_ref[...]).astype(BF16)
    for hh in range(MLA_HEADS):
        sl = slice(hh * LANES, (hh + 1) * LANES)
        q_ref[:, sl] = ((q[:, sl] * cos + qs[:, sl] * sin) * scale).astype(BF16)
        k_ref[:, sl] = (k[:, sl] + kr).astype(BF16)


def mla_prep(x, g, w1, qn, kvn, wq, wqs, wk, wv, cos_t, sin_t, *, tm, seq):
    n, d = x.shape
    width = MLA_HEADS * LANES
    steps_per_seq = seq // tm
    full = lambda a: pl.BlockSpec(a.shape, lambda i: (0,) * a.ndim)
    out = jax.ShapeDtypeStruct((n, width), BF16)
    row = pl.BlockSpec((tm, width), lambda i: (i, 0))
    args = (g.reshape(1, d), w1, qn.reshape(1, -1), kvn.reshape(1, -1), wq, wqs, wk, wv)
    return pl.pallas_call(
        functools.partial(_mla_prep_kernel, scale=(MLA_NOPE + MLA_ROPE) ** -0.5),
        out_shape=(out, out, out),
        grid=(n // tm,),
        in_specs=[pl.BlockSpec((tm, d), lambda i: (i, 0))] + [full(a) for a in args] + [
            pl.BlockSpec((tm, LANES), lambda i: (i % steps_per_seq, 0)),
            pl.BlockSpec((tm, LANES), lambda i: (i % steps_per_seq, 0)),
        ],
        out_specs=(row, row, row),
        compiler_params=_params("parallel"),
        name="mla_prep",
    )(x, *args, cos_t, sin_t)


def _flash_kernel(q_ref, k_ref, v_ref, o_ref, s_scr, p_scr, alpha_scr, m_scr, l_scr, acc_scr, *, t, ts,
                  q_scale, slopes, heads_per_slope):
    qi = pl.program_id(2)
    m_scr[...] = jnp.full(m_scr.shape, -jnp.inf, F32)
    l_scr[...] = jnp.zeros(l_scr.shape, F32)
    acc_scr[...] = jnp.zeros(acc_scr.shape, F32)
    alpha_scr[1] = jnp.ones(alpha_scr.shape[1:], F32)
    p_scr[1] = jnp.zeros(p_scr.shape[1:], BF16)
    q = q_ref[...]
    if q_scale != 1.0:
        q = q * q_scale

    if slopes is not None:
        slope_idx = pl.program_id(1) // heads_per_slope
        slope = jnp.float32(slopes[-1])
        for i in range(len(slopes) - 2, -1, -1):
            slope = jnp.where(slope_idx == i, jnp.float32(slopes[i]), slope)
        col = lax.broadcasted_iota(jnp.int32, (1, t), 1)
    lane_minus_row = (lax.broadcasted_iota(jnp.int32, (ts, LANES), 1)
                      - lax.broadcasted_iota(jnp.int32, (ts, LANES), 0))

    def scores(kb):
        s = _dot_nt(q, k_ref[pl.ds(pl.multiple_of(kb * t, t), t), :])
        if slopes is not None:
            s = s + slope * (col + (kb - qi) * t).astype(F32)
        s_scr[kb & 1] = s

    def weighted_values(kb, slot):
        pv = _dot(p_scr[slot], v_ref[pl.ds(pl.multiple_of(kb * t, t), t), :])
        acc_scr[...] = alpha_scr[slot] * acc_scr[...] + pv

    def softmax(slot, diagonal):
        for u in range(t // ts):
            rows = slice(u * ts, (u + 1) * ts)
            nkeys = -(-(u + 1) * ts // LANES) * LANES if diagonal else t
            sc = [s_scr[slot, rows, c * LANES:(c + 1) * LANES] for c in range(nkeys // LANES)]
            if diagonal:
                for c in range(u * ts // LANES, nkeys // LANES):
                    sc[c] = jnp.where(lane_minus_row <= u * ts - c * LANES, sc[c], -jnp.inf)
            m_prev = m_scr[rows, :]
            m_new = jnp.maximum(m_prev, jnp.max(functools.reduce(jnp.maximum, sc), axis=1, keepdims=True))
            alpha = jnp.exp(m_prev - m_new)
            ps = [jnp.exp(x - m_new) for x in sc]
            l_scr[rows, :] = alpha * l_scr[rows, :] + jnp.sum(functools.reduce(jnp.add, ps), axis=1, keepdims=True)
            for c, pc in enumerate(ps):
                p_scr[slot, rows, c * LANES:(c + 1) * LANES] = pc.astype(BF16)
            if nkeys < t:
                p_scr[slot, rows, nkeys:] = jnp.zeros((ts, t - nkeys), BF16)
            alpha_scr[slot, rows, :] = alpha
            m_scr[rows, :] = m_new

    scores(0)

    def body(kb, carry):
        slot = kb & 1
        weighted_values(jnp.maximum(kb - 1, 0), 1 - slot)
        softmax(slot, False)
        scores(kb + 1)
        return carry

    lax.fori_loop(0, qi, body, 0)
    slot = qi & 1
    weighted_values(jnp.maximum(qi - 1, 0), 1 - slot)
    softmax(slot, True)
    weighted_values(qi, slot)
    o_ref[...] = (acc_scr[...] / l_scr[...]).astype(o_ref.dtype)


def flash_attention(q, k, v, *, batch, seq, heads, q_col0, k_col0, v_col0, heads_per_kv, t, ts,
                    q_scale=1.0, slopes=None, heads_per_slope=1, out_dtype=BF16):
    nq = seq // t
    kernel = functools.partial(_flash_kernel, t=t, ts=ts, q_scale=q_scale, slopes=slopes,
                               heads_per_slope=heads_per_slope)
    return pl.pallas_call(
        kernel,
        out_shape=jax.ShapeDtypeStruct((batch * seq, heads * LANES), out_dtype),
        grid=(batch, heads, nq),
        in_specs=[
            pl.BlockSpec((t, LANES), lambda b, h, i: (b * nq + i, q_col0 + h)),
            pl.BlockSpec((seq, LANES), lambda b, h, i: (b, k_col0 + h // heads_per_kv)),
            pl.BlockSpec((seq, LANES), lambda b, h, i: (b, v_col0 + h // heads_per_kv)),
        ],
        out_specs=pl.BlockSpec((t, LANES), lambda b, h, i: (b * nq + i, h)),
        scratch_shapes=[pltpu.VMEM((2, t, t), F32), pltpu.VMEM((2, t, t), BF16),
                        pltpu.VMEM((2, t, LANES), F32)] + [pltpu.VMEM((t, LANES), F32)] * 3,
        compiler_params=_params("parallel", "parallel", "arbitrary"),
        name="flash_attention",
    )(q, k, v)


def _sb_kernel(q_ref, k_ref, v_ref, tri_ref, o_ref, z_scr, p_scr, c_scr, acc_scr, *, tq, tk, ts, q_scale):
    qi = pl.program_id(2)
    ratio = tq // tk
    nchunk = tk // LANES
    q = q_ref[...] * q_scale
    tri = tri_ref[...]
    c_scr[...] = jnp.zeros(c_scr.shape, F32)
    acc_scr[...] = jnp.zeros(acc_scr.shape, F32)
    p_scr[1] = jnp.zeros(p_scr.shape[1:], BF16)
    lane_minus_row = (lax.broadcasted_iota(jnp.int32, (ts, LANES), 1)
                      - lax.broadcasted_iota(jnp.int32, (ts, LANES), 0))

    def scores(kc, slot):
        z_scr[slot] = _dot_nt(q, k_ref[pl.ds(pl.multiple_of(kc * tk, tk), tk), :])

    def weighted_values(kc, slot):
        acc_scr[...] += _dot(p_scr[slot], v_ref[pl.ds(pl.multiple_of(kc * tk, tk), tk), :])

    def weights(slot, local):
        for u in range(tq // ts):
            rows = slice(u * ts, (u + 1) * ts)
            zs = [z_scr[slot, rows, c * LANES:(c + 1) * LANES] for c in range(nchunk)]
            lss = [jnp.minimum(z, 0.0) - jnp.log(1.0 + jnp.exp(-jnp.abs(z))) for z in zs]
            lks = [ls - z for ls, z in zip(lss, zs)]
            if local is not None:
                valid = [lane_minus_row < u * ts - local - c * LANES for c in range(nchunk)]
                lks = [jnp.where(m, lk, 0.0) for m, lk in zip(valid, lks)]
            after = _dot(jnp.concatenate(lks, axis=1).astype(BF16), tri)
            c_prev = c_scr[rows, :]
            for c in range(nchunk):
                p = jnp.exp(lss[c] + after[:, c * LANES:(c + 1) * LANES] + c_prev)
                if local is not None:
                    p = jnp.where(valid[c], p, 0.0)
                p_scr[slot, rows, c * LANES:(c + 1) * LANES] = p.astype(BF16)
            c_scr[rows, :] = c_prev + jnp.sum(functools.reduce(jnp.add, lks), axis=1, keepdims=True)

    n = (qi + 1) * ratio
    scores(n - 1, 0)
    for j in range(ratio):
        weighted_values(n - max(j, 1), 1 - j % 2)
        weights(j % 2, (ratio - 1 - j) * tk)
        scores(jnp.maximum(n - 2 - j, 0), 1 - j % 2)

    def body(i, carry):
        kc = qi * ratio - 1 - i
        slot = (ratio + i) & 1
        weighted_values(kc + 1, 1 - slot)
        weights(slot, None)
        scores(jnp.maximum(kc - 1, 0), 1 - slot)
        return carry

    lax.fori_loop(0, qi * ratio, body, 0)
    weighted_values(0, (n - 1) & 1)
    o_ref[...] = acc_scr[...].astype(o_ref.dtype)


def sb_attention(q, k, v, *, batch, seq, heads, q_col0, k_col0, v_col0, tq, tk, ts, q_scale):
    nq = seq // tq
    r = lax.broadcasted_iota(jnp.int32, (tk, tk), 0)
    c = lax.broadcasted_iota(jnp.int32, (tk, tk), 1)
    tri = (r > c).astype(BF16)
    return pl.pallas_call(
        functools.partial(_sb_kernel, tq=tq, tk=tk, ts=ts, q_scale=q_scale),
        out_shape=jax.ShapeDtypeStruct((batch * seq, heads * LANES), BF16),
        grid=(batch, heads, nq),
        in_specs=[
            pl.BlockSpec((tq, LANES), lambda b, h, i: (b * nq + i, q_col0 + h)),
            pl.BlockSpec((seq, LANES), lambda b, h, i: (b, k_col0 + h)),
            pl.BlockSpec((seq, LANES), lambda b, h, i: (b, v_col0 + h)),
            pl.BlockSpec((tk, tk), lambda b, h, i: (0, 0)),
        ],
        out_specs=pl.BlockSpec((tq, LANES), lambda b, h, i: (b * nq + i, h)),
        scratch_shapes=[pltpu.VMEM((2, tq, tk), F32), pltpu.VMEM((2, tq, tk), BF16),
                        pltpu.VMEM((tq, LANES), F32), pltpu.VMEM((tq, LANES), F32)],
        compiler_params=_params("parallel", "parallel", "arbitrary"),
        name="sb_attention",
    )(q, k, v, tri)


def _merge_kernel(x_ref, g_ref, wg_ref, oa_ref, ob_ref, oc_ref, lam_ref, subln_ref,
                  wba_ref, wbb_ref, wbc_ref, wo_ref, o_ref, *, lam_init):
    x = x_ref[...]
    d = x.shape[1]
    h = _rms(x, g_ref[...]).astype(BF16)
    lp = lam_ref[...]
    lam = (jnp.exp(jnp.sum(lp[0:1] * lp[1:2], axis=1, keepdims=True))
           - jnp.exp(jnp.sum(lp[2:3] * lp[3:4], axis=1, keepdims=True)) + lam_init)
    oc = oc_ref[...]
    parts = []
    for hh in range(DIFF_HEADS):
        p1 = oc[:, (2 * hh) * LANES:(2 * hh + 1) * LANES]
        p2 = oc[:, (2 * hh + 1) * LANES:(2 * hh + 2) * LANES]
        parts.append((_rms(p1 - lam * p2, subln_ref[...]) * (1.0 - lam_init)).astype(BF16))
    ocn = jnp.concatenate(parts, axis=1)

    def gate(i):
        return jax.nn.sigmoid(_dot(h, wg_ref[:, i * d:(i + 1) * d]))

    merged = gate(0) * _dot(oa_ref[...], wba_ref[...])
    merged += gate(1) * _dot(ob_ref[...], wbb_ref[...])
    merged += gate(2) * _dot(ocn, wbc_ref[...])
    o_ref[...] = x + _dot(merged.astype(BF16), wo_ref[...])


def merge(x, g, wg, oa, ob, oc, lam_p, subln, wba, wbb, wbc, wo, *, lam_init, tm):
    n, d = x.shape
    full = lambda a: pl.BlockSpec(a.shape, lambda i: (0,) * a.ndim)
    row = lambda a: pl.BlockSpec((tm, a.shape[1]), lambda i: (i, 0))
    g2 = g.reshape(1, d)
    subln2 = subln.reshape(1, -1)
    return pl.pallas_call(
        functools.partial(_merge_kernel, lam_init=lam_init),
        out_shape=jax.ShapeDtypeStruct((n, d), F32),
        grid=(n // tm,),
        in_specs=[row(x), full(g2), full(wg), row(oa), row(ob), row(oc), full(lam_p), full(subln2),
                  full(wba), full(wbb), full(wbc), full(wo)],
        out_specs=row(x),
        compiler_params=_params("parallel"),
        name="merge",
    )(x, g2, wg, oa, ob, oc, lam_p, subln2, wba, wbb, wbc, wo)


def _cross_kernel(x_ref, g_ref, wq_ref, k_ref, v_ref, wo_ref, o_ref, *, scale):
    x = x_ref[...]
    h = _rms(x, g_ref[...]).astype(BF16)
    q = (_dot(h, wq_ref[...]) * scale).astype(BF16)
    outs = []
    for hh in range(X_HEADS):
        sl = slice(hh * X_DIM, (hh + 1) * X_DIM)
        s = _dot_nt(q[:, sl], k_ref[:, sl])
        p = jnp.exp(s - jnp.max(s, axis=1, keepdims=True))
        o = _dot(p.astype(BF16), v_ref[:, sl]) / jnp.sum(p, axis=1, keepdims=True)
        outs.append(o.astype(BF16))
    o_ref[...] = x + _dot(jnp.concatenate(outs, axis=1), wo_ref[...])


def cross_attention(x, g, wq, kv, wo, *, batch, seq, tm):
    n, d = x.shape
    n_mem = kv.shape[0] // batch
    width = X_HEADS * X_DIM
    steps = seq // tm
    full = lambda a: pl.BlockSpec(a.shape, lambda b, i: (0,) * a.ndim)
    g2 = g.reshape(1, d)
    return pl.pallas_call(
        functools.partial(_cross_kernel, scale=X_DIM ** -0.5),
        out_shape=jax.ShapeDtypeStruct((n, d), F32),
        grid=(batch, steps),
        in_specs=[
            pl.BlockSpec((tm, d), lambda b, i: (b * steps + i, 0)),
            full(g2), full(wq),
            pl.BlockSpec((n_mem, width), lambda b, i: (b, 0)),
            pl.BlockSpec((n_mem, width), lambda b, i: (b, 1)),
            full(wo),
        ],
        out_specs=pl.BlockSpec((tm, d), lambda b, i: (b * steps + i, 0)),
        compiler_params=_params("parallel", "parallel"),
        name="cross_attention",
    )(x, g2, wq, kv, kv, wo)


def _router_kernel(x_ref, g_ref, w_ref, b_ref, comb_ref):
    h = _rms(x_ref[...], g_ref[...])
    logits = jnp.dot(h, w_ref[...], preferred_element_type=F32, precision=lax.Precision.HIGHEST) + b_ref[...]
    lane = lax.broadcasted_iota(jnp.int32, logits.shape, 1).astype(F32)
    neg = -jnp.inf
    far = float(4 * LANES)

    def first_argmax(vals, vmax):
        return jnp.min(jnp.where(vals == vmax, lane, far), axis=1, keepdims=True)

    gl = jnp.where((lane >= GROUP_LANE0) & (lane < GROUP_LANE0 + N_GROUPS), logits, neg)
    gmax = jnp.max(gl, axis=1, keepdims=True)
    lo = (first_argmax(gl, gmax) - GROUP_LANE0) * EXPERTS_PER_GROUP
    p_g = 1.0 / jnp.sum(jnp.exp(gl - gmax), axis=1, keepdims=True)
    el = jnp.where((lane >= lo) & (lane < lo + EXPERTS_PER_GROUP), logits, neg)
    v1 = jnp.max(el, axis=1, keepdims=True)
    i1 = first_argmax(el, v1)
    el2 = jnp.where(lane == i1, neg, el)
    v2 = jnp.max(el2, axis=1, keepdims=True)
    i2 = first_argmax(el2, v2)
    t = jnp.exp(v2 - v1)
    w1 = 1.0 / (1.0 + t)
    comb_ref[...] = jnp.where(lane == i1, w1 * p_g, 0.0) + jnp.where(lane == i2, (t * w1) * p_g, 0.0)


def router(x, g, w, b, *, tm):
    n, d = x.shape
    full = lambda a: pl.BlockSpec(a.shape, lambda i: (0,) * a.ndim)
    g2 = g.reshape(1, d)
    return pl.pallas_call(
        _router_kernel,
        out_shape=jax.ShapeDtypeStruct((n, LANES), F32),
        grid=(n // tm,),
        in_specs=[pl.BlockSpec((tm, d), lambda i: (i, 0)), full(g2), full(w), full(b)],
        out_specs=pl.BlockSpec((tm, LANES), lambda i: (i, 0)),
        compiler_params=_params("parallel"),
        name="router",
    )(x, g2, w, b)


def _moe_kernel(x_ref, g_ref, comb_ref, wg_ref, wu_ref, wd_ref, fg_ref, o_ref, h_scr, y_scr, *, ec, final_norm):
    j = pl.program_id(1)

    @pl.when(j == 0)
    def _():
        h_scr[...] = _rms(x_ref[...], g_ref[...]).astype(BF16)
        y_scr[...] = jnp.zeros(y_scr.shape, F32)

    h = h_scr[...]
    comb = comb_ref[...]
    lane = lax.broadcasted_iota(jnp.int32, comb.shape, 1)
    for e in range(ec):
        a = _dot(h, wg_ref[e])
        u = _dot(h, wu_ref[e])
        cw = jnp.sum(jnp.where(lane == j * ec + e, comb, 0.0), axis=1, keepdims=True)
        act = (a * jax.nn.sigmoid(a)) * u * cw
        y_scr[...] += _dot(act.astype(BF16), wd_ref[e])

    @pl.when(j == pl.num_programs(1) - 1)
    def _():
        y = x_ref[...] + y_scr[...]
        if final_norm:
            y = _rms(y, fg_ref[...])
        o_ref[...] = y


def moe(x, g, comb, wg, wu, wd, fg, *, tm, ec, final_norm):
    n, d = x.shape
    n_exp, _, f = wg.shape
    g2 = g.reshape(1, d)
    fg2 = fg.reshape(1, d)
    return pl.pallas_call(
        functools.partial(_moe_kernel, ec=ec, final_norm=final_norm),
        out_shape=jax.ShapeDtypeStruct((n, d), F32),
        grid=(n // tm, n_exp // ec),
        in_specs=[
            pl.BlockSpec((tm, d), lambda i, j: (i, 0)),
            pl.BlockSpec((1, d), lambda i, j: (0, 0)),
            pl.BlockSpec((tm, LANES), lambda i, j: (i, 0)),
            pl.BlockSpec((ec, d, f), lambda i, j: (j, 0, 0)),
            pl.BlockSpec((ec, d, f), lambda i, j: (j, 0, 0)),
            pl.BlockSpec((ec, f, d), lambda i, j: (j, 0, 0)),
            pl.BlockSpec((1, d), lambda i, j: (0, 0)),
        ],
        out_specs=pl.BlockSpec((tm, d), lambda i, j: (i, 0)),
        scratch_shapes=[pltpu.VMEM((tm, d), BF16), pltpu.VMEM((tm, d), F32)],
        compiler_params=_params("parallel", "arbitrary"),
        name="moe",
    )(x, g2, comb, wg, wu, wd, fg2)


def _pad_heads(w, heads, dim, lane0=0):
    k = w.shape[0]
    w = w.reshape(k, heads, dim)
    w = jnp.pad(w, ((0, 0), (0, 0), (lane0, LANES - dim - lane0)))
    return w.reshape(k, heads * LANES)


def _pad_head_rows(w, heads, dim):
    m = w.shape[1]
    w = w.reshape(heads, dim, m)
    w = jnp.pad(w, ((0, 0), (0, LANES - dim), (0, 0)))
    return w.reshape(heads * LANES, m)


def _swap_halves(w):
    half = w.shape[-1] // 2
    return jnp.concatenate([w[..., half:], w[..., :half]], axis=-1)


def _rope_tables(seq):
    half = MLA_ROPE // 2
    inv_freq = ROPE_THETA ** (-jnp.arange(half, dtype=F32) / half)
    ang = jnp.arange(seq, dtype=F32)[:, None] * inv_freq[None, :]
    cos, sin = jnp.cos(ang), jnp.sin(ang)
    ones = jnp.ones((seq, MLA_NOPE), F32)
    zn = jnp.zeros((seq, MLA_NOPE), F32)
    zp = jnp.zeros((seq, LANES - MLA_NOPE - MLA_ROPE), F32)
    return (jnp.concatenate([ones, cos, cos, zp], axis=1),
            jnp.concatenate([zn, -sin, sin, zp], axis=1))


def kernel(x, mem, attn_norm, w_in, q_norm, w_uq, kv_norm, w_ukv, diff_lambda, diff_subln, w_branch, w_out,
           cross_norm, mem_norm, w_xq, w_xkv, w_xo, ffn_norm, w_router_grp, b_router_grp, w_router_exp,
           b_router_exp, w_exp_gate, w_exp_up, w_exp_down, final_norm):
    batch, seq, d = x.shape
    depth = w_in.shape[0]
    n = batch * seq
    n_mem = mem.shape[1]
    tm = min(512, seq)
    t_attn = min(512, seq)
    ts_attn = min(64, seq)
    tk_sb = min(256, seq)

    gate_cols = 3 * d
    o_cq = gate_cols
    o_ckv = o_cq + MLA_Q_RANK
    o_kr = o_ckv + MLA_KV_RANK
    o_sb = o_kr + MLA_ROPE
    sb_w = SB_HEADS * SB_DIM
    o_diff = o_sb + 3 * sb_w
    diff_qk = DIFF_HEADS * 2 * DIFF_DIM

    cos_t, sin_t = _rope_tables(seq)
    xf = x.reshape(n, d)
    mem_f = mem.reshape(batch * n_mem, d)

    for l in range(depth):
        wl = w_in[l]
        kr_w = wl[:, o_kr:o_sb]
        w1 = jnp.concatenate([
            wl[:, o_cq:o_kr],
            jnp.pad(kr_w, ((0, 0), (MLA_NOPE, LANES - MLA_NOPE - MLA_ROPE))),
            jnp.pad(_swap_halves(kr_w), ((0, 0), (MLA_NOPE, LANES - MLA_NOPE - MLA_ROPE))),
        ], axis=1).astype(BF16)
        uq = w_uq[l].reshape(MLA_Q_RANK, MLA_HEADS, MLA_NOPE + MLA_ROPE)
        uq_nope, uq_rope = uq[..., :MLA_NOPE], uq[..., MLA_NOPE:]
        padq = ((0, 0), (0, 0), (0, LANES - MLA_NOPE - MLA_ROPE))
        wq = jnp.pad(jnp.concatenate([uq_nope, uq_rope], -1), padq).reshape(MLA_Q_RANK, -1).astype(BF16)
        wqs = jnp.pad(jnp.concatenate([jnp.zeros_like(uq_nope), _swap_halves(uq_rope)], -1), padq)
        wqs = wqs.reshape(MLA_Q_RANK, -1).astype(BF16)
        ukv = w_ukv[l].reshape(MLA_KV_RANK, MLA_HEADS, MLA_NOPE + MLA_V)
        wk = _pad_heads(ukv[..., :MLA_NOPE].reshape(MLA_KV_RANK, -1), MLA_HEADS, MLA_NOPE).astype(BF16)
        wv = _pad_heads(ukv[..., MLA_NOPE:].reshape(MLA_KV_RANK, -1), MLA_HEADS, MLA_V).astype(BF16)
        q_a, k_a, v_a = mla_prep(xf, attn_norm[l], w1, q_norm[l], kv_norm[l], wq, wqs, wk, wv,
                                 cos_t, sin_t, tm=tm, seq=seq)

        sb = wl[:, o_sb:o_diff]
        dq = wl[:, o_diff:o_diff + diff_qk].reshape(d, DIFF_HEADS, 2, DIFF_DIM)
        dq = jnp.stack([jnp.pad(dq[:, :, 0], ((0, 0), (0, 0), (0, DIFF_DIM))),
                        jnp.pad(dq[:, :, 1], ((0, 0), (0, 0), (DIFF_DIM, 0)))], axis=2)
        w2 = jnp.concatenate([
            _pad_heads(sb[:, :sb_w], SB_HEADS, SB_DIM),
            _pad_heads(sb[:, sb_w:2 * sb_w], SB_HEADS, SB_DIM),
            _pad_heads(sb[:, 2 * sb_w:], SB_HEADS, SB_DIM),
            dq.reshape(d, 2 * DIFF_HEADS * LANES),
            wl[:, o_diff + diff_qk:o_diff + 3 * diff_qk],
        ], axis=1).astype(BF16)
        p2 = norm_matmul(xf, attn_norm[l], w2, tm=tm, tn=512)
        cb_sb_q, cb_sb_k, cb_sb_v = 0, SB_HEADS, 2 * SB_HEADS
        cb_d_q = 3 * SB_HEADS
        cb_d_k = cb_d_q + 2 * DIFF_HEADS
        cb_d_v = cb_d_k + DIFF_HEADS

        o_a = flash_attention(q_a, k_a, v_a, batch=batch, seq=seq, heads=MLA_HEADS, q_col0=0, k_col0=0,
                              v_col0=0, heads_per_kv=1, t=t_attn, ts=ts_attn)
        o_b = sb_attention(p2, p2, p2, batch=batch, seq=seq, heads=SB_HEADS, q_col0=cb_sb_q,
                           k_col0=cb_sb_k, v_col0=cb_sb_v, tq=t_attn, tk=tk_sb, ts=min(256, seq),
                           q_scale=SB_DIM ** -0.5)
        slopes = tuple(2.0 ** (-8.0 * (i + 1) / DIFF_HEADS) for i in range(DIFF_HEADS))
        o_c = flash_attention(p2, p2, p2, batch=batch, seq=seq, heads=2 * DIFF_HEADS, q_col0=cb_d_q,
                              k_col0=cb_d_k, v_col0=cb_d_v, heads_per_kv=2, t=t_attn, ts=ts_attn,
                              q_scale=DIFF_DIM ** -0.5, slopes=slopes, heads_per_slope=2, out_dtype=F32)

        lam_init = 0.8 - 0.6 * math.exp(-0.3 * l)
        wba = _pad_head_rows(w_branch[l, 0], MLA_HEADS, MLA_V).astype(BF16)
        wbb = _pad_head_rows(w_branch[l, 1], SB_HEADS, SB_DIM).astype(BF16)
        xf = merge(xf, attn_norm[l], wl[:, :gate_cols].astype(BF16), o_a, o_b, o_c, diff_lambda[l],
                   diff_subln[l], wba, wbb, w_branch[l, 2].astype(BF16), w_out[l].astype(BF16),
                   lam_init=lam_init, tm=min(256, seq))

        kv = norm_matmul(mem_f, mem_norm, w_xkv[l].astype(BF16), tm=min(256, n_mem), tn=512)
        xf = cross_attention(xf, cross_norm[l], w_xq[l].astype(BF16), kv, w_xo[l].astype(BF16),
                             batch=batch, seq=seq, tm=tm)

        pad_r = LANES - N_EXPERTS - N_GROUPS
        wr = jnp.pad(jnp.concatenate([w_router_exp[l], w_router_grp[l]], axis=1), ((0, 0), (0, pad_r)))
        br = jnp.pad(jnp.concatenate([b_router_exp[l], b_router_grp[l]]), (0, pad_r)).reshape(1, LANES)
        comb = router(xf, ffn_norm[l], wr, br, tm=tm)
        xf = moe(xf, ffn_norm[l], comb, w_exp_gate[l].astype(BF16), w_exp_up[l].astype(BF16),
                 w_exp_down[l].astype(BF16), final_norm, tm=tm, ec=4, final_norm=(l == depth - 1))

    return xf.reshape(batch, seq, d)
```

```python
import functools
import math

import jax
import jax.numpy as jnp
from jax import lax
from jax.experimental import pallas as pl
from jax.experimental.pallas import tpu as pltpu

F32 = jnp.float32
BF16 = jnp.bfloat16
EPS = 1e-6
LANES = 128
LOG2E = math.log2(math.e)
VMEM_LIMIT_BYTES = 56 * 1024 * 1024

ROPE_THETA = 10000.0
MLA_HEADS, MLA_NOPE, MLA_ROPE, MLA_V = 8, 64, 32, 64
MLA_Q_RANK, MLA_KV_RANK = 256, 256
SB_HEADS, SB_DIM = 8, 64
DIFF_HEADS, DIFF_DIM = 4, 64
X_HEADS, X_DIM = 4, 128
N_GROUPS, EXPERTS_PER_GROUP = 4, 8
N_EXPERTS = N_GROUPS * EXPERTS_PER_GROUP
GROUP_LANE0 = N_EXPERTS


def _params(*semantics):
    return pltpu.CompilerParams(dimension_semantics=semantics, vmem_limit_bytes=VMEM_LIMIT_BYTES)


def _rms(x, g):
    return x * lax.rsqrt(jnp.mean(x * x, axis=-1, keepdims=True) + EPS) * g


def _dot(a, b):
    return jnp.dot(a, b, preferred_element_type=F32)


def _dot_nt(a, b):
    return lax.dot_general(a, b, (((1,), (1,)), ((), ())), preferred_element_type=F32)


def _norm_matmul_kernel(x_ref, g_ref, w_ref, o_ref, h_scr):
    @pl.when(pl.program_id(1) == 0)
    def _():
        h_scr[...] = _rms(x_ref[...], g_ref[...]).astype(BF16)

    o_ref[...] = _dot(h_scr[...], w_ref[...]).astype(o_ref.dtype)


def norm_matmul(x, g, w, *, tm, tn, out_dtype=BF16):
    n, k = x.shape
    m = w.shape[1]
    return pl.pallas_call(
        _norm_matmul_kernel,
        out_shape=jax.ShapeDtypeStruct((n, m), out_dtype),
        grid=(n // tm, m // tn),
        in_specs=[
            pl.BlockSpec((tm, k), lambda i, j: (i, 0)),
            pl.BlockSpec((1, k), lambda i, j: (0, 0)),
            pl.BlockSpec((k, tn), lambda i, j: (0, j)),
        ],
        out_specs=pl.BlockSpec((tm, tn), lambda i, j: (i, j)),
        scratch_shapes=[pltpu.VMEM((tm, k), BF16)],
        compiler_params=_params("parallel", "arbitrary"),
        name="norm_matmul",
    )(x, g.reshape(1, k), w)


def _mla_prep_kernel(x_ref, g_ref, w1_ref, qn_ref, kvn_ref, wq_ref, wqs_ref, wk_ref, wv_ref,
                     cos_ref, sin_ref, q_ref, k_ref, v_ref, *, scale):
    h = _rms(x_ref[...], g_ref[...]).astype(BF16)
    c = _dot(h, w1_ref[...])
    cq = _rms(c[:, :MLA_Q_RANK], qn_ref[...]).astype(BF16)
    ckv = _rms(c[:, MLA_Q_RANK:MLA_Q_RANK + MLA_KV_RANK], kvn_ref[...]).astype(BF16)
    base = MLA_Q_RANK + MLA_KV_RANK
    cos = cos_ref[...]
    sin = sin_ref[...]
    kr = c[:, base:base + LANES] * cos + c[:, base + LANES:base + 2 * LANES] * sin
    q = _dot(cq, wq_ref[...])
    qs = _dot(cq, wqs_ref[...])
    k = _dot(ckv, wk_ref[...])
    v_ref[...] = _dot(ckv, wv_ref[...]).astype(BF16)
    for hh in range(MLA_HEADS):
        sl = slice(hh * LANES, (hh + 1) * LANES)
        q_ref[:, sl] = ((q[:, sl] * cos + qs[:, sl] * sin) * scale).astype(BF16)
        k_ref[:, sl] = (k[:, sl] + kr).astype(BF16)


def mla_prep(x, g, w1, qn, kvn, wq, wqs, wk, wv, cos_t, sin_t, *, tm, seq):
    n, d = x.shape
    width = MLA_HEADS * LANES
    steps_per_seq = seq // tm
    full = lambda a: pl.BlockSpec(a.shape, lambda i: (0,) * a.ndim)
    out = jax.ShapeDtypeStruct((n, width), BF16)
    row = pl.BlockSpec((tm, width), lambda i: (i, 0))
    args = (g.reshape(1, d), w1, qn.reshape(1, -1), kvn.reshape(1, -1), wq, wqs, wk, wv)
    return pl.pallas_call(
        functools.partial(_mla_prep_kernel, scale=(MLA_NOPE + MLA_ROPE) ** -0.5 * LOG2E),
        out_shape=(out, out, out),
        grid=(n // tm,),
        in_specs=[pl.BlockSpec((tm, d), lambda i: (i, 0))] + [full(a) for a in args] + [
            pl.BlockSpec((tm, LANES), lambda i: (i % steps_per_seq, 0)),
            pl.BlockSpec((tm, LANES), lambda i: (i % steps_per_seq, 0)),
        ],
        out_specs=(row, row, row),
        compiler_params=_params("parallel"),
        name="mla_prep",
    )(x, *args, cos_t, sin_t)


def _flash_kernel(q_ref, k_ref, v_ref, o_ref, s_scr, p_scr, alpha_scr, m_scr, l_scr, acc_scr, *, tq, tk, ts,
                  unroll, q_scale, slopes, heads_per_slope):
    qi = pl.program_id(2)
    ratio = tq // tk
    nchunk = tk // LANES
    m_scr[...] = jnp.full(m_scr.shape, -jnp.inf, F32)
    l_scr[...] = jnp.zeros(l_scr.shape, F32)
    acc_scr[...] = jnp.zeros(acc_scr.shape, F32)
    alpha_scr[1] = jnp.ones(alpha_scr.shape[1:], F32)
    p_scr[1] = jnp.zeros(p_scr.shape[1:], BF16)
    q = q_ref[...]
    if q_scale != 1.0:
        q = (q.astype(F32) * q_scale).astype(BF16)

    if slopes is not None:
        slope_idx = pl.program_id(1) // heads_per_slope
        slope = jnp.float32(slopes[-1])
        for i in range(len(slopes) - 2, -1, -1):
            slope = jnp.where(slope_idx == i, jnp.float32(slopes[i]), slope)
        col = lax.broadcasted_iota(jnp.int32, (1, tk), 1)
    lane_minus_row = (lax.broadcasted_iota(jnp.int32, (ts, LANES), 1)
                      - lax.broadcasted_iota(jnp.int32, (ts, LANES), 0))

    def scores(kb, slot):
        s = _dot_nt(q, k_ref[pl.ds(pl.multiple_of(kb * tk, tk), tk), :])
        if slopes is not None:
            s = s + (slope * LOG2E) * (col + (kb * tk - qi * tq)).astype(F32)
        s_scr[slot] = s

    def weighted_values(kb, slot):
        pv = _dot(p_scr[slot], v_ref[pl.ds(pl.multiple_of(kb * tk, tk), tk), :])
        acc_scr[...] = alpha_scr[slot] * acc_scr[...] + pv

    def softmax(s_slot, slot, local):
        for u in range(tq // ts):
            rows = slice(u * ts, (u + 1) * ts)
            last = u * ts + ts - 1
            live = [c for c in range(nchunk) if local is None or local + c * LANES <= last]
            if not live:
                p_scr[slot, rows, :] = jnp.zeros((ts, tk), BF16)
                alpha_scr[slot, rows, :] = jnp.ones((ts, LANES), F32)
                continue
            sc = [s_scr[s_slot, rows, c * LANES:(c + 1) * LANES] for c in live]
            if local is not None:
                for i, c in enumerate(live):
                    if local + c * LANES + LANES - 1 > u * ts:
                        sc[i] = jnp.where(lane_minus_row <= u * ts - local - c * LANES, sc[i], -jnp.inf)
            m_prev = m_scr[rows, :]
            m_new = jnp.maximum(m_prev, jnp.max(functools.reduce(jnp.maximum, sc), axis=1, keepdims=True))
            alpha = jnp.exp2(m_prev - m_new)
            ps = [jnp.exp2(x - m_new) for x in sc]
            l_scr[rows, :] = alpha * l_scr[rows, :] + jnp.sum(functools.reduce(jnp.add, ps), axis=1, keepdims=True)
            for c, pc in zip(live, ps):
                p_scr[slot, rows, c * LANES:(c + 1) * LANES] = pc.astype(BF16)
            if len(live) < nchunk:
                p_scr[slot, rows, len(live) * LANES:] = jnp.zeros((ts, tk - len(live) * LANES), BF16)
            alpha_scr[slot, rows, :] = alpha
            m_scr[rows, :] = m_new

    def trip(kb, slot, local, p_slot=None, last=False):
        weighted_values(jnp.maximum(kb - 1, 0), 1 - slot)
        softmax(slot, slot if p_slot is None else p_slot, local)
        if not last:
            scores(kb + 1, 1 - slot)

    scores(0, 0)
    n_full = qi * ratio
    if unroll == 2:
        def body(i, carry):
            trip(2 * i, 0, None)
            trip(2 * i + 1, 1, None)
            return carry

        lax.fori_loop(0, n_full // 2, body, 0)
    else:
        def body(kb, carry):
            trip(kb, kb & 1, None)
            return carry

        lax.fori_loop(0, n_full, body, 0)
    if ratio == 1:
        slot = n_full & 1
        trip(n_full, slot, 0, last=True)
        weighted_values(n_full, slot)
    else:
        trip(n_full, 0, 0)
        trip(n_full + 1, 1, tk, p_slot=2, last=True)
        weighted_values(n_full + 1, 2)
    o_ref[...] = (acc_scr[...] / l_scr[...]).astype(o_ref.dtype)


def flash_attention(q, k, v, *, batch, seq, heads, q_col0, k_col0, v_col0, heads_per_kv, tq, tk, ts,
                    unroll=1, q_scale=1.0, slopes=None, heads_per_slope=1, out_dtype=BF16):
    nq = seq // tq
    assert tq // tk in (1, 2) and tq % tk == 0 and (unroll == 1 or tq // tk == 2)
    kernel = functools.partial(_flash_kernel, tq=tq, tk=tk, ts=ts, unroll=unroll, q_scale=q_scale, slopes=slopes,
                               heads_per_slope=heads_per_slope)
    return pl.pallas_call(
        kernel,
        out_shape=jax.ShapeDtypeStruct((batch * seq, heads * LANES), out_dtype),
        grid=(batch, heads, nq),
        in_specs=[
            pl.BlockSpec((tq, LANES), lambda b, h, i: (b * nq + i, q_col0 + h)),
            pl.BlockSpec((seq, LANES), lambda b, h, i: (b, k_col0 + h // heads_per_kv)),
            pl.BlockSpec((seq, LANES), lambda b, h, i: (b, v_col0 + h // heads_per_kv)),
        ],
        out_specs=pl.BlockSpec((tq, LANES), lambda b, h, i: (b * nq + i, h)),
        scratch_shapes=[pltpu.VMEM((2, tq, tk), F32), pltpu.VMEM((3, tq, tk), BF16),
                        pltpu.VMEM((3, tq, LANES), F32)] + [pltpu.VMEM((tq, LANES), F32)] * 3,
        compiler_params=_params("parallel", "parallel", "arbitrary"),
        name="flash_attention",
    )(q, k, v)


def _sb_kernel(q_ref, k_ref, v_ref, tri_ref, o_ref, z_scr, p_scr, c_scr, acc_scr, *, tq, tk, ts, q_scale):
    qi = pl.program_id(2)
    ratio = tq // tk
    nchunk = tk // LANES
    q = (q_ref[...].astype(F32) * q_scale).astype(BF16)
    tri = tri_ref[...]
    c_scr[...] = jnp.zeros(c_scr.shape, F32)
    acc_scr[...] = jnp.zeros(acc_scr.shape, F32)
    p_scr[1] = jnp.zeros(p_scr.shape[1:], BF16)
    lane_minus_row = (lax.broadcasted_iota(jnp.int32, (ts, LANES), 1)
                      - lax.broadcasted_iota(jnp.int32, (ts, LANES), 0))

    def scores(kc, slot):
        z_scr[slot] = _dot_nt(q, k_ref[pl.ds(pl.multiple_of(kc * tk, tk), tk), :])

    def weighted_values(kc, slot):
        acc_scr[...] += _dot(p_scr[slot], v_ref[pl.ds(pl.multiple_of(kc * tk, tk), tk), :])

    def weights(slot, local):
        for u in range(tq // ts):
            rows = slice(u * ts, (u + 1) * ts)
            zs = [z_scr[slot, rows, c * LANES:(c + 1) * LANES] for c in range(nchunk)]
            lss = [jnp.minimum(z, 0.0) - jnp.log2(1.0 + jnp.exp2(-jnp.abs(z))) for z in zs]
            lks = [ls - z for ls, z in zip(lss, zs)]
            if local is not None:
                valid = [lane_minus_row < u * ts - local - c * LANES for c in range(nchunk)]
                lks = [jnp.where(m, lk, 0.0) for m, lk in zip(valid, lks)]
            after = _dot(jnp.concatenate(lks, axis=1).astype(BF16), tri)
            c_prev = c_scr[rows, :]
            for c in range(nchunk):
                p = jnp.exp2(lss[c] + after[:, c * LANES:(c + 1) * LANES] + c_prev)
                if local is not None:
                    p = jnp.where(valid[c], p, 0.0)
                p_scr[slot, rows, c * LANES:(c + 1) * LANES] = p.astype(BF16)
            c_scr[rows, :] = c_prev + jnp.sum(functools.reduce(jnp.add, lks), axis=1, keepdims=True)

    n = (qi + 1) * ratio
    scores(n - 1, 0)
    for j in range(ratio):
        weighted_values(n - max(j, 1), 1 - j % 2)
        weights(j % 2, (ratio - 1 - j) * tk)
        scores(jnp.maximum(n - 2 - j, 0), 1 - j % 2)

    def trip(kc, slot):
        weighted_values(kc + 1, 1 - slot)
        weights(slot, None)
        scores(jnp.maximum(kc - 1, 0), 1 - slot)

    def body(i, carry):
        kc = qi * ratio - 1 - 2 * i
        trip(kc, 0)
        trip(kc - 1, 1)
        return carry

    lax.fori_loop(0, qi, body, 0)
    weighted_values(0, 1)
    o_ref[...] = acc_scr[...].astype(o_ref.dtype)


def sb_attention(q, k, v, *, batch, seq, heads, q_col0, k_col0, v_col0, tq, tk, ts, q_scale):
    assert tq == 2 * tk
    nq = seq // tq
    r = lax.broadcasted_iota(jnp.int32, (tk, tk), 0)
    c = lax.broadcasted_iota(jnp.int32, (tk, tk), 1)
    tri = (r > c).astype(BF16)
    return pl.pallas_call(
        functools.partial(_sb_kernel, tq=tq, tk=tk, ts=ts, q_scale=q_scale),
        out_shape=jax.ShapeDtypeStruct((batch * seq, heads * LANES), BF16),
        grid=(batch, heads, nq),
        in_specs=[
            pl.BlockSpec((tq, LANES), lambda b, h, i: (b * nq + i, q_col0 + h)),
            pl.BlockSpec((seq, LANES), lambda b, h, i: (b, k_col0 + h)),
            pl.BlockSpec((seq, LANES), lambda b, h, i: (b, v_col0 + h)),
            pl.BlockSpec((tk, tk), lambda b, h, i: (0, 0)),
        ],
        out_specs=pl.BlockSpec((tq, LANES), lambda b, h, i: (b * nq + i, h)),
        scratch_shapes=[pltpu.VMEM((2, tq, tk), F32), pltpu.VMEM((2, tq, tk), BF16),
                        pltpu.VMEM((tq, LANES), F32), pltpu.VMEM((tq, LANES), F32)],
        compiler_params=_params("parallel", "parallel", "arbitrary"),
        name="sb_attention",
    )(q, k, v, tri)


def _merge_kernel(x_ref, g_ref, wg_ref, oa_ref, ob_ref, oc_ref, lam_ref, subln_ref,
                  wba_ref, wbb_ref, wbc_ref, wo_ref, o_ref, *, lam_init):
    x = x_ref[...]
    d = x.shape[1]
    h = _rms(x, g_ref[...]).astype(BF16)
    lp = lam_ref[...]
    lam = (jnp.exp(jnp.sum(lp[0:1] * lp[1:2], axis=1, keepdims=True))
           - jnp.exp(jnp.sum(lp[2:3] * lp[3:4], axis=1, keepdims=True)) + lam_init)
    oc = oc_ref[...]
    parts = []
    for hh in range(DIFF_HEADS):
        p1 = oc[:, (2 * hh) * LANES:(2 * hh + 1) * LANES]
        p2 = oc[:, (2 * hh + 1) * LANES:(2 * hh + 2) * LANES]
        parts.append((_rms(p1 - lam * p2, subln_ref[...]) * (1.0 - lam_init)).astype(BF16))
    ocn = jnp.concatenate(parts, axis=1)

    def gate(i):
        return jax.nn.sigmoid(_dot(h, wg_ref[:, i * d:(i + 1) * d]))

    merged = gate(0) * _dot(oa_ref[...], wba_ref[...])
    merged += gate(1) * _dot(ob_ref[...], wbb_ref[...])
    merged += gate(2) * _dot(ocn, wbc_ref[...])
    o_ref[...] = x + _dot(merged.astype(BF16), wo_ref[...])


def merge(x, g, wg, oa, ob, oc, lam_p, subln, wba, wbb, wbc, wo, *, lam_init, tm):
    n, d = x.shape
    full = lambda a: pl.BlockSpec(a.shape, lambda i: (0,) * a.ndim)
    row = lambda a: pl.BlockSpec((tm, a.shape[1]), lambda i: (i, 0))
    g2 = g.reshape(1, d)
    subln2 = subln.reshape(1, -1)
    return pl.pallas_call(
        functools.partial(_merge_kernel, lam_init=lam_init),
        out_shape=jax.ShapeDtypeStruct((n, d), F32),
        grid=(n // tm,),
        in_specs=[row(x), full(g2), full(wg), row(oa), row(ob), row(oc), full(lam_p), full(subln2),
                  full(wba), full(wbb), full(wbc), full(wo)],
        out_specs=row(x),
        compiler_params=_params("parallel"),
        name="merge",
    )(x, g2, wg, oa, ob, oc, lam_p, subln2, wba, wbb, wbc, wo)


def _cross_kernel(x_ref, g_ref, wq_ref, k_ref, v_ref, wo_ref, o_ref, *, scale):
    x = x_ref[...]
    h = _rms(x, g_ref[...]).astype(BF16)
    q = (_dot(h, wq_ref[...]) * scale).astype(BF16)
    outs = []
    for hh in range(X_HEADS):
        sl = slice(hh * X_DIM, (hh + 1) * X_DIM)
        s = _dot_nt(q[:, sl], k_ref[:, sl])
        p = jnp.exp(s - jnp.max(s, axis=1, keepdims=True))
        o = _dot(p.astype(BF16), v_ref[:, sl]) / jnp.sum(p, axis=1, keepdims=True)
        outs.append(o.astype(BF16))
    o_ref[...] = x + _dot(jnp.concatenate(outs, axis=1), wo_ref[...])


def cross_attention(x, g, wq, kv, wo, *, batch, seq, tm):
    n, d = x.shape
    n_mem = kv.shape[0] // batch
    width = X_HEADS * X_DIM
    steps = seq // tm
    full = lambda a: pl.BlockSpec(a.shape, lambda b, i: (0,) * a.ndim)
    g2 = g.reshape(1, d)
    return pl.pallas_call(
        functools.partial(_cross_kernel, scale=X_DIM ** -0.5),
        out_shape=jax.ShapeDtypeStruct((n, d), F32),
        grid=(batch, steps),
        in_specs=[
            pl.BlockSpec((tm, d), lambda b, i: (b * steps + i, 0)),
            full(g2), full(wq),
            pl.BlockSpec((n_mem, width), lambda b, i: (b, 0)),
            pl.BlockSpec((n_mem, width), lambda b, i: (b, 1)),
            full(wo),
        ],
        out_specs=pl.BlockSpec((tm, d), lambda b, i: (b * steps + i, 0)),
        compiler_params=_params("parallel", "parallel"),
        name="cross_attention",
    )(x, g2, wq, kv, kv, wo)


def _router_kernel(x_ref, g_ref, w_ref, b_ref, comb_ref):
    h = _rms(x_ref[...], g_ref[...])
    logits = jnp.dot(h, w_ref[...], preferred_element_type=F32, precision=lax.Precision.HIGHEST) + b_ref[...]
    lane = lax.broadcasted_iota(jnp.int32, logits.shape, 1).astype(F32)
    neg = -jnp.inf
    far = float(4 * LANES)

    def first_argmax(vals, vmax):
        return jnp.min(jnp.where(vals == vmax, lane, far), axis=1, keepdims=True)

    gl = jnp.where((lane >= GROUP_LANE0) & (lane < GROUP_LANE0 + N_GROUPS), logits, neg)
    gmax = jnp.max(gl, axis=1, keepdims=True)
    lo = (first_argmax(gl, gmax) - GROUP_LANE0) * EXPERTS_PER_GROUP
    p_g = 1.0 / jnp.sum(jnp.exp(gl - gmax), axis=1, keepdims=True)
    el = jnp.where((lane >= lo) & (lane < lo + EXPERTS_PER_GROUP), logits, neg)
    v1 = jnp.max(el, axis=1, keepdims=True)
    i1 = first_argmax(el, v1)
    el2 = jnp.where(lane == i1, neg, el)
    v2 = jnp.max(el2, axis=1, keepdims=True)
    i2 = first_argmax(el2, v2)
    t = jnp.exp(v2 - v1)
    w1 = 1.0 / (1.0 + t)
    comb_ref[...] = jnp.where(lane == i1, w1 * p_g, 0.0) + jnp.where(lane == i2, (t * w1) * p_g, 0.0)


def router(x, g, w, b, *, tm):
    n, d = x.shape
    full = lambda a: pl.BlockSpec(a.shape, lambda i: (0,) * a.ndim)
    g2 = g.reshape(1, d)
    return pl.pallas_call(
        _router_kernel,
        out_shape=jax.ShapeDtypeStruct((n, LANES), F32),
        grid=(n // tm,),
        in_specs=[pl.BlockSpec((tm, d), lambda i: (i, 0)), full(g2), full(w), full(b)],
        out_specs=pl.BlockSpec((tm, LANES), lambda i: (i, 0)),
        compiler_params=_params("parallel"),
        name="router",
    )(x, g2, w, b)


def _moe_kernel(x_ref, g_ref, comb_ref, wg_ref, wu_ref, wd_ref, fg_ref, o_ref, h_scr, y_scr, *, ec, final_norm):
    j = pl.program_id(1)

    @pl.when(j == 0)
    def _():
        h_scr[...] = _rms(x_ref[...], g_ref[...]).astype(BF16)
        y_scr[...] = jnp.zeros(y_scr.shape, F32)

    h = h_scr[...]
    comb = comb_ref[...]
    lane = lax.broadcasted_iota(jnp.int32, comb.shape, 1)
    for e in range(ec):
        a = _dot(h, wg_ref[e])
        u = _dot(h, wu_ref[e])
        cw = jnp.sum(jnp.where(lane == j * ec + e, comb, 0.0), axis=1, keepdims=True)
        act = (a * jax.nn.sigmoid(a)) * u * cw
        y_scr[...] += _dot(act.astype(BF16), wd_ref[e])

    @pl.when(j == pl.num_programs(1) - 1)
    def _():
        y = x_ref[...] + y_scr[...]
        if final_norm:
            y = _rms(y, fg_ref[...])
        o_ref[...] = y


def moe(x, g, comb, wg, wu, wd, fg, *, tm, ec, final_norm):
    n, d = x.shape
    n_exp, _, f = wg.shape
    g2 = g.reshape(1, d)
    fg2 = fg.reshape(1, d)
    return pl.pallas_call(
        functools.partial(_moe_kernel, ec=ec, final_norm=final_norm),
        out_shape=jax.ShapeDtypeStruct((n, d), F32),
        grid=(n // tm, n_exp // ec),
        in_specs=[
            pl.BlockSpec((tm, d), lambda i, j: (i, 0)),
            pl.BlockSpec((1, d), lambda i, j: (0, 0)),
            pl.BlockSpec((tm, LANES), lambda i, j: (i, 0)),
            pl.BlockSpec((ec, d, f), lambda i, j: (j, 0, 0)),
            pl.BlockSpec((ec, d, f), lambda i, j: (j, 0, 0)),
            pl.BlockSpec((ec, f, d), lambda i, j: (j, 0, 0)),
            pl.BlockSpec((1, d), lambda i, j: (0, 0)),
        ],
        out_specs=pl.BlockSpec((tm, d), lambda i, j: (i, 0)),
        scratch_shapes=[pltpu.VMEM((tm, d), BF16), pltpu.VMEM((tm, d), F32)],
        compiler_params=_params("parallel", "arbitrary"),
        name="moe",
    )(x, g2, comb, wg, wu, wd, fg2)


def _pad_heads(w, heads, dim, lane0=0):
    k = w.shape[0]
    w = w.reshape(k, heads, dim)
    w = jnp.pad(w, ((0, 0), (0, 0), (lane0, LANES - dim - lane0)))
    return w.reshape(k, heads * LANES)


def _pad_head_rows(w, heads, dim):
    m = w.shape[1]
    w = w.reshape(heads, dim, m)
    w = jnp.pad(w, ((0, 0), (0, LANES - dim), (0, 0)))
    return w.reshape(heads * LANES, m)


def _swap_halves(w):
    half = w.shape[-1] // 2
    return jnp.concatenate([w[..., half:], w[..., :half]], axis=-1)


def _rope_tables(seq):
    half = MLA_ROPE // 2
    inv_freq = ROPE_THETA ** (-jnp.arange(half, dtype=F32) / half)
    ang = jnp.arange(seq, dtype=F32)[:, None] * inv_freq[None, :]
    cos, sin = jnp.cos(ang), jnp.sin(ang)
    ones = jnp.ones((seq, MLA_NOPE), F32)
    zn = jnp.zeros((seq, MLA_NOPE), F32)
    zp = jnp.zeros((seq, LANES - MLA_NOPE - MLA_ROPE), F32)
    return (jnp.concatenate([ones, cos, cos, zp], axis=1),
            jnp.concatenate([zn, -sin, sin, zp], axis=1))


def kernel(x, mem, attn_norm, w_in, q_norm, w_uq, kv_norm, w_ukv, diff_lambda, diff_subln, w_branch, w_out,
           cross_norm, mem_norm, w_xq, w_xkv, w_xo, ffn_norm, w_router_grp, b_router_grp, w_router_exp,
           b_router_exp, w_exp_gate, w_exp_up, w_exp_down, final_norm):
    batch, seq, d = x.shape
    depth = w_in.shape[0]
    n = batch * seq
    n_mem = mem.shape[1]
    tm = min(512, seq)
    tq_attn = min(1024, seq)
    tk_attn = tq_attn // 2
    ts_attn = min(64, seq)
    ts_sb = min(256, seq)

    gate_cols = 3 * d
    o_cq = gate_cols
    o_ckv = o_cq + MLA_Q_RANK
    o_kr = o_ckv + MLA_KV_RANK
    o_sb = o_kr + MLA_ROPE
    sb_w = SB_HEADS * SB_DIM
    o_diff = o_sb + 3 * sb_w
    diff_qk = DIFF_HEADS * 2 * DIFF_DIM

    cos_t, sin_t = _rope_tables(seq)
    xf = x.reshape(n, d)
    mem_f = mem.reshape(batch * n_mem, d)

    for l in range(depth):
        wl = w_in[l]
        kr_w = wl[:, o_kr:o_sb]
        w1 = jnp.concatenate([
            wl[:, o_cq:o_kr],
            jnp.pad(kr_w, ((0, 0), (MLA_NOPE, LANES - MLA_NOPE - MLA_ROPE))),
            jnp.pad(_swap_halves(kr_w), ((0, 0), (MLA_NOPE, LANES - MLA_NOPE - MLA_ROPE))),
        ], axis=1).astype(BF16)
        uq = w_uq[l].reshape(MLA_Q_RANK, MLA_HEADS, MLA_NOPE + MLA_ROPE)
        uq_nope, uq_rope = uq[..., :MLA_NOPE], uq[..., MLA_NOPE:]
        padq = ((0, 0), (0, 0), (0, LANES - MLA_NOPE - MLA_ROPE))
        wq = jnp.pad(jnp.concatenate([uq_nope, uq_rope], -1), padq).reshape(MLA_Q_RANK, -1).astype(BF16)
        wqs = jnp.pad(jnp.concatenate([jnp.zeros_like(uq_nope), _swap_halves(uq_rope)], -1), padq)
        wqs = wqs.reshape(MLA_Q_RANK, -1).astype(BF16)
        ukv = w_ukv[l].reshape(MLA_KV_RANK, MLA_HEADS, MLA_NOPE + MLA_V)
        wk = _pad_heads(ukv[..., :MLA_NOPE].reshape(MLA_KV_RANK, -1), MLA_HEADS, MLA_NOPE).astype(BF16)
        wv = _pad_heads(ukv[..., MLA_NOPE:].reshape(MLA_KV_RANK, -1), MLA_HEADS, MLA_V).astype(BF16)
        q_a, k_a, v_a = mla_prep(xf, attn_norm[l], w1, q_norm[l], kv_norm[l], wq, wqs, wk, wv,
                                 cos_t, sin_t, tm=tm, seq=seq)

        sb = wl[:, o_sb:o_diff]
        dq = wl[:, o_diff:o_diff + diff_qk].reshape(d, DIFF_HEADS, 2, DIFF_DIM)
        dq = jnp.stack([jnp.pad(dq[:, :, 0], ((0, 0), (0, 0), (0, DIFF_DIM))),
                        jnp.pad(dq[:, :, 1], ((0, 0), (0, 0), (DIFF_DIM, 0)))], axis=2)
        w2 = jnp.concatenate([
            _pad_heads(sb[:, :sb_w], SB_HEADS, SB_DIM),
            _pad_heads(sb[:, sb_w:2 * sb_w], SB_HEADS, SB_DIM),
            _pad_heads(sb[:, 2 * sb_w:], SB_HEADS, SB_DIM),
            dq.reshape(d, 2 * DIFF_HEADS * LANES),
            wl[:, o_diff + diff_qk:o_diff + 3 * diff_qk],
        ], axis=1).astype(BF16)
        p2 = norm_matmul(xf, attn_norm[l], w2, tm=tm, tn=512)
        cb_sb_q, cb_sb_k, cb_sb_v = 0, SB_HEADS, 2 * SB_HEADS
        cb_d_q = 3 * SB_HEADS
        cb_d_k = cb_d_q + 2 * DIFF_HEADS
        cb_d_v = cb_d_k + DIFF_HEADS

        o_a = flash_attention(q_a, k_a, v_a, batch=batch, seq=seq, heads=MLA_HEADS, q_col0=0, k_col0=0,
                              v_col0=0, heads_per_kv=1, tq=tq_attn, tk=tk_attn, ts=ts_attn, unroll=2)
        o_b = sb_attention(p2, p2, p2, batch=batch, seq=seq, heads=SB_HEADS, q_col0=cb_sb_q,
                           k_col0=cb_sb_k, v_col0=cb_sb_v, tq=tq_attn, tk=tk_attn, ts=ts_sb,
                           q_scale=SB_DIM ** -0.5 * LOG2E)
        slopes = tuple(2.0 ** (-8.0 * (i + 1) / DIFF_HEADS) for i in range(DIFF_HEADS))
        o_c = flash_attention(p2, p2, p2, batch=batch, seq=seq, heads=2 * DIFF_HEADS, q_col0=cb_d_q,
                              k_col0=cb_d_k, v_col0=cb_d_v, heads_per_kv=2, tq=tq_attn, tk=tk_attn, ts=ts_attn,
                              unroll=2, q_scale=DIFF_DIM ** -0.5 * LOG2E, slopes=slopes, heads_per_slope=2, out_dtype=F32)

        lam_init = 0.8 - 0.6 * math.exp(-0.3 * l)
        wba = _pad_head_rows(w_branch[l, 0], MLA_HEADS, MLA_V).astype(BF16)
        wbb = _pad_head_rows(w_branch[l, 1], SB_HEADS, SB_DIM).astype(BF16)
        xf = merge(xf, attn_norm[l], wl[:, :gate_cols].astype(BF16), o_a, o_b, o_c, diff_lambda[l],
                   diff_subln[l], wba, wbb, w_branch[l, 2].astype(BF16), w_out[l].astype(BF16),
                   lam_init=lam_init, tm=min(256, seq))

        kv = norm_matmul(mem_f, mem_norm, w_xkv[l].astype(BF16), tm=min(256, n_mem), tn=512)
        xf = cross_attention(xf, cross_norm[l], w_xq[l].astype(BF16), kv, w_xo[l].astype(BF16),
                             batch=batch, seq=seq, tm=tm)

        pad_r = LANES - N_EXPERTS - N_GROUPS
        wr = jnp.pad(jnp.concatenate([w_router_exp[l], w_router_grp[l]], axis=1), ((0, 0), (0, pad_r)))
        br = jnp.pad(jnp.concatenate([b_router_exp[l], b_router_grp[l]]), (0, pad_r)).reshape(1, LANES)
        comb = router(xf, ffn_norm[l], wr, br, tm=tm)
        xf = moe(xf, ffn_norm[l], comb, w_exp_gate[l].astype(BF16), w_exp_up[l].astype(BF16),
                 w_exp_down[l].astype(BF16), final_norm, tm=tm, ec=4, final_norm=(l == depth - 1))

    return xf.reshape(batch, seq, d)
```

```python
import functools
import math

import jax
import jax.numpy as jnp
from jax import lax
from jax.experimental import pallas as pl
from jax.experimental.pallas import tpu as pltpu

F32 = jnp.float32
BF16 = jnp.bfloat16
EPS = 1e-6
LANES = 128
LOG2E = math.log2(math.e)
VMEM_LIMIT_BYTES = 56 * 1024 * 1024

ROPE_THETA = 10000.0
MLA_HEADS, MLA_NOPE, MLA_ROPE, MLA_V = 8, 64, 32, 64
MLA_Q_RANK, MLA_KV_RANK = 256, 256
SB_HEADS, SB_DIM = 8, 64
DIFF_HEADS, DIFF_DIM = 4, 64
X_HEADS, X_DIM = 4, 128
N_GROUPS, EXPERTS_PER_GROUP = 4, 8
N_EXPERTS = N_GROUPS * EXPERTS_PER_GROUP
GROUP_LANE0 = N_EXPERTS


def _params(*semantics):
    return pltpu.CompilerParams(dimension_semantics=semantics, vmem_limit_bytes=VMEM_LIMIT_BYTES)


def _rms(x, g):
    return x * lax.rsqrt(jnp.mean(x * x, axis=-1, keepdims=True) + EPS) * g


def _dot(a, b):
    return jnp.dot(a, b, preferred_element_type=F32)


def _dot_nt(a, b):
    return lax.dot_general(a, b, (((1,), (1,)), ((), ())), preferred_element_type=F32)


def _norm_matmul_kernel(x_ref, g_ref, w_ref, o_ref, h_scr):
    @pl.when(pl.program_id(1) == 0)
    def _():
        h_scr[...] = _rms(x_ref[...], g_ref[...]).astype(BF16)

    o_ref[...] = _dot(h_scr[...], w_ref[...]).astype(o_ref.dtype)


def norm_matmul(x, g, w, *, tm, tn, out_dtype=BF16):
    n, k = x.shape
    m = w.shape[1]
    return pl.pallas_call(
        _norm_matmul_kernel,
        out_shape=jax.ShapeDtypeStruct((n, m), out_dtype),
        grid=(n // tm, m // tn),
        in_specs=[
            pl.BlockSpec((tm, k), lambda i, j: (i, 0)),
            pl.BlockSpec((1, k), lambda i, j: (0, 0)),
            pl.BlockSpec((k, tn), lambda i, j: (0, j)),
        ],
        out_specs=pl.BlockSpec((tm, tn), lambda i, j: (i, j)),
        scratch_shapes=[pltpu.VMEM((tm, k), BF16)],
        compiler_params=_params("parallel", "arbitrary"),
        name="norm_matmul",
    )(x, g.reshape(1, k), w)


def _mla_prep_kernel(x_ref, g_ref, w1_ref, qn_ref, kvn_ref, wq_ref, wqs_ref, wk_ref, wv_ref,
                     cos_ref, sin_ref, q_ref, k_ref, v_ref, *, scale):
    h = _rms(x_ref[...], g_ref[...]).astype(BF16)
    c = _dot(h, w1_ref[...])
    cq = _rms(c[:, :MLA_Q_RANK], qn_ref[...]).astype(BF16)
    ckv = _rms(c[:, MLA_Q_RANK:MLA_Q_RANK + MLA_KV_RANK], kvn_ref[...]).astype(BF16)
    base = MLA_Q_RANK + MLA_KV_RANK
    cos = cos_ref[...]
    sin = sin_ref[...]
    kr = c[:, base:base + LANES] * cos + c[:, base + LANES:base + 2 * LANES] * sin
    q = _dot(cq, wq_ref[...])
    qs = _dot(cq, wqs_ref[...])
    k = _dot(ckv, wk_ref[...])
    v_ref[...] = _dot(ckv, wv_ref[...]).astype(BF16)
    for hh in range(MLA_HEADS):
        sl = slice(hh * LANES, (hh + 1) * LANES)
        q_ref[:, sl] = ((q[:, sl] * cos + qs[:, sl] * sin) * scale).astype(BF16)
        k_ref[:, sl] = (k[:, sl] + kr).astype(BF16)


def mla_prep(x, g, w1, qn, kvn, wq, wqs, wk, wv, cos_t, sin_t, *, tm, seq):
    n, d = x.shape
    width = MLA_HEADS * LANES
    steps_per_seq = seq // tm
    full = lambda a: pl.BlockSpec(a.shape, lambda i: (0,) * a.ndim)
    out = jax.ShapeDtypeStruct((n, width), BF16)
    row = pl.BlockSpec((tm, width), lambda i: (i, 0))
    args = (g.reshape(1, d), w1, qn.reshape(1, -1), kvn.reshape(1, -1), wq, wqs, wk, wv)
    return pl.pallas_call(
        functools.partial(_mla_prep_kernel, scale=(MLA_NOPE + MLA_ROPE) ** -0.5 * LOG2E),
        out_shape=(out, out, out),
        grid=(n // tm,),
        in_specs=[pl.BlockSpec((tm, d), lambda i: (i, 0))] + [full(a) for a in args] + [
            pl.BlockSpec((tm, LANES), lambda i: (i % steps_per_seq, 0)),
            pl.BlockSpec((tm, LANES), lambda i: (i % steps_per_seq, 0)),
        ],
        out_specs=(row, row, row),
        compiler_params=_params("parallel"),
        name="mla_prep",
    )(x, *args, cos_t, sin_t)


def _flash_kernel(q_ref, k_ref, v_ref, o_ref, s_scr, p_scr, alpha_scr, m_scr, l_scr, acc_scr, *, tq, tk, ts,
                  unroll, q_scale, slopes, heads_per_slope):
    qi = pl.program_id(2)
    ratio = tq // tk
    nchunk = tk // LANES
    m_scr[...] = jnp.full(m_scr.shape, -jnp.inf, F32)
    l_scr[...] = jnp.zeros(l_scr.shape, F32)
    acc_scr[...] = jnp.zeros(acc_scr.shape, F32)
    alpha_scr[1] = jnp.ones(alpha_scr.shape[1:], F32)
    p_scr[1] = jnp.zeros(p_scr.shape[1:], BF16)
    q = q_ref[...]
    if q_scale != 1.0:
        q = (q.astype(F32) * q_scale).astype(BF16)

    if slopes is not None:
        slope_idx = pl.program_id(1) // heads_per_slope
        slope = jnp.float32(slopes[-1])
        for i in range(len(slopes) - 2, -1, -1):
            slope = jnp.where(slope_idx == i, jnp.float32(slopes[i]), slope)
        col = lax.broadcasted_iota(jnp.int32, (1, tk), 1)
    lane_minus_row = (lax.broadcasted_iota(jnp.int32, (ts, LANES), 1)
                      - lax.broadcasted_iota(jnp.int32, (ts, LANES), 0))

    def scores(kb, slot):
        s = _dot_nt(q, k_ref[pl.ds(pl.multiple_of(kb * tk, tk), tk), :])
        if slopes is not None:
            s = s + (slope * LOG2E) * (col + (kb * tk - qi * tq)).astype(F32)
        s_scr[slot] = s

    def weighted_values(kb, slot):
        pv = _dot(p_scr[slot], v_ref[pl.ds(pl.multiple_of(kb * tk, tk), tk), :])
        acc_scr[...] = alpha_scr[slot] * acc_scr[...] + pv

    def softmax(s_slot, slot, local):
        for u in range(tq // ts):
            rows = slice(u * ts, (u + 1) * ts)
            last = u * ts + ts - 1
            live = [c for c in range(nchunk) if local is None or local + c * LANES <= last]
            if not live:
                p_scr[slot, rows, :] = jnp.zeros((ts, tk), BF16)
                alpha_scr[slot, rows, :] = jnp.ones((ts, LANES), F32)
                continue
            sc = [s_scr[s_slot, rows, c * LANES:(c + 1) * LANES] for c in live]
            if local is not None:
                for i, c in enumerate(live):
                    if local + c * LANES + LANES - 1 > u * ts:
                        sc[i] = jnp.where(lane_minus_row <= u * ts - local - c * LANES, sc[i], -jnp.inf)
            m_prev = m_scr[rows, :]
            m_new = jnp.maximum(m_prev, jnp.max(functools.reduce(jnp.maximum, sc), axis=1, keepdims=True))
            alpha = jnp.exp2(m_prev - m_new)
            ps = [jnp.exp2(x - m_new) for x in sc]
            l_scr[rows, :] = alpha * l_scr[rows, :] + jnp.sum(functools.reduce(jnp.add, ps), axis=1, keepdims=True)
            for c, pc in zip(live, ps):
                p_scr[slot, rows, c * LANES:(c + 1) * LANES] = pc.astype(BF16)
            if len(live) < nchunk:
                p_scr[slot, rows, len(live) * LANES:] = jnp.zeros((ts, tk - len(live) * LANES), BF16)
            alpha_scr[slot, rows, :] = alpha
            m_scr[rows, :] = m_new

    def trip(kb, slot, local, p_slot=None, last=False):
        weighted_values(jnp.maximum(kb - 1, 0), 1 - slot)
        softmax(slot, slot if p_slot is None else p_slot, local)
        if not last:
            scores(kb + 1, 1 - slot)

    scores(0, 0)
    n_full = qi * ratio
    if unroll == 2:
        def body(i, carry):
            trip(2 * i, 0, None)
            trip(2 * i + 1, 1, None)
            return carry

        lax.fori_loop(0, n_full // 2, body, 0)
    else:
        def body(kb, carry):
            trip(kb, kb & 1, None)
            return carry

        lax.fori_loop(0, n_full, body, 0)
    if ratio == 1:
        slot = n_full & 1
        trip(n_full, slot, 0, last=True)
        weighted_values(n_full, slot)
    else:
        trip(n_full, 0, 0)
        trip(n_full + 1, 1, tk, p_slot=2, last=True)
        weighted_values(n_full + 1, 2)
    o_ref[...] = (acc_scr[...] / l_scr[...]).astype(o_ref.dtype)


def flash_attention(q, k, v, *, batch, seq, heads, q_col0, k_col0, v_col0, heads_per_kv, tq, tk, ts,
                    unroll=1, q_scale=1.0, slopes=None, heads_per_slope=1, out_dtype=BF16):
    nq = seq // tq
    assert tq // tk in (1, 2) and tq % tk == 0 and (unroll == 1 or tq // tk == 2)
    kernel = functools.partial(_flash_kernel, tq=tq, tk=tk, ts=ts, unroll=unroll, q_scale=q_scale, slopes=slopes,
                               heads_per_slope=heads_per_slope)
    return pl.pallas_call(
        kernel,
        out_shape=jax.ShapeDtypeStruct((batch * seq, heads * LANES), out_dtype),
        grid=(batch, heads, nq),
        in_specs=[
            pl.BlockSpec((tq, LANES), lambda b, h, i: (b * nq + i, q_col0 + h)),
            pl.BlockSpec((seq, LANES), lambda b, h, i: (b, k_col0 + h // heads_per_kv)),
            pl.BlockSpec((seq, LANES), lambda b, h, i: (b, v_col0 + h // heads_per_kv)),
        ],
        out_specs=pl.BlockSpec((tq, LANES), lambda b, h, i: (b * nq + i, h)),
        scratch_shapes=[pltpu.VMEM((2, tq, tk), F32), pltpu.VMEM((3, tq, tk), BF16),
                        pltpu.VMEM((3, tq, LANES), F32)] + [pltpu.VMEM((tq, LANES), F32)] * 3,
        compiler_params=_params("parallel", "parallel", "arbitrary"),
        name="flash_attention",
    )(q, k, v)


def _sb_kernel(q_ref, k_ref, v_ref, tri_ref, o_ref, z_scr, p_scr, c_scr, acc_scr, *, tq, tk, tw, ts, q_scale):
    qi = pl.program_id(2)
    ratio = tq // tk
    nchunk = tk // LANES
    q = (q_ref[...].astype(F32) * q_scale).astype(BF16)
    tri = tri_ref[...]
    c_scr[...] = jnp.zeros(c_scr.shape, F32)
    acc_scr[...] = jnp.zeros(acc_scr.shape, F32)
    p_scr[1] = jnp.zeros(p_scr.shape[1:], BF16)
    lane_minus_row = (lax.broadcasted_iota(jnp.int32, (ts, LANES), 1)
                      - lax.broadcasted_iota(jnp.int32, (ts, LANES), 0))

    def scores(kc, slot):
        z_scr[slot] = _dot_nt(q, k_ref[pl.ds(pl.multiple_of(kc * tk, tk), tk), :])

    def weighted_values(kc, slot):
        acc_scr[...] += _dot(p_scr[slot], v_ref[pl.ds(pl.multiple_of(kc * tk, tk), tk), :])

    def weights(slot, local):
        for u in range(tq // ts):
            rows = slice(u * ts, (u + 1) * ts)
            zs = [z_scr[slot, rows, c * LANES:(c + 1) * LANES] for c in range(nchunk)]
            lss = [jnp.minimum(z, 0.0) - jnp.log2(1.0 + jnp.exp2(-jnp.abs(z))) for z in zs]
            lks = [ls - z for ls, z in zip(lss, zs)]
            if local is not None:
                valid = [lane_minus_row < u * ts - local - c * LANES for c in range(nchunk)]
                lks = [jnp.where(m, lk, 0.0) for m, lk in zip(valid, lks)]
            later = c_scr[rows, :]
            gc = tw // LANES
            for g in range(nchunk // gc - 1, -1, -1):
                grp = lks[g * gc:(g + 1) * gc]
                after = jnp.concatenate([later] * gc, axis=1) + _dot(jnp.concatenate(grp, axis=1).astype(BF16), tri)
                for i in range(gc):
                    c = g * gc + i
                    p = jnp.exp2(lss[c] + after[:, i * LANES:(i + 1) * LANES])
                    if local is not None:
                        p = jnp.where(valid[c], p, 0.0)
                    p_scr[slot, rows, c * LANES:(c + 1) * LANES] = p.astype(BF16)
                later = later + jnp.sum(functools.reduce(jnp.add, grp), axis=1, keepdims=True)
            c_scr[rows, :] = later

    n = (qi + 1) * ratio
    scores(n - 1, 0)
    for j in range(ratio):
        weighted_values(n - max(j, 1), 1 - j % 2)
        weights(j % 2, (ratio - 1 - j) * tk)
        scores(jnp.maximum(n - 2 - j, 0), 1 - j % 2)

    def trip(kc, slot):
        weighted_values(kc + 1, 1 - slot)
        weights(slot, None)
        scores(jnp.maximum(kc - 1, 0), 1 - slot)

    def body(i, carry):
        kc = qi * ratio - 1 - 2 * i
        trip(kc, 0)
        trip(kc - 1, 1)
        return carry

    lax.fori_loop(0, qi, body, 0)
    weighted_values(0, 1)
    o_ref[...] = acc_scr[...].astype(o_ref.dtype)


def sb_attention(q, k, v, *, batch, seq, heads, q_col0, k_col0, v_col0, tq, tk, tw, ts, q_scale):
    assert tq == 2 * tk and tk % tw == 0
    nq = seq // tq
    r = lax.broadcasted_iota(jnp.int32, (tw, tw), 0)
    c = lax.broadcasted_iota(jnp.int32, (tw, tw), 1)
    tri = (r > c).astype(BF16)
    return pl.pallas_call(
        functools.partial(_sb_kernel, tq=tq, tk=tk, tw=tw, ts=ts, q_scale=q_scale),
        out_shape=jax.ShapeDtypeStruct((batch * seq, heads * LANES), BF16),
        grid=(batch, heads, nq),
        in_specs=[
            pl.BlockSpec((tq, LANES), lambda b, h, i: (b * nq + i, q_col0 + h)),
            pl.BlockSpec((seq, LANES), lambda b, h, i: (b, k_col0 + h)),
            pl.BlockSpec((seq, LANES), lambda b, h, i: (b, v_col0 + h)),
            pl.BlockSpec((tw, tw), lambda b, h, i: (0, 0)),
        ],
        out_specs=pl.BlockSpec((tq, LANES), lambda b, h, i: (b * nq + i, h)),
        scratch_shapes=[pltpu.VMEM((2, tq, tk), F32), pltpu.VMEM((2, tq, tk), BF16),
                        pltpu.VMEM((tq, LANES), F32), pltpu.VMEM((tq, LANES), F32)],
        compiler_params=_params("parallel", "parallel", "arbitrary"),
        name="sb_attention",
    )(q, k, v, tri)


def _merge_kernel(x_ref, g_ref, wg_ref, oa_ref, ob_ref, oc_ref, lam_ref, subln_ref,
                  wba_ref, wbb_ref, wbc_ref, wo_ref, o_ref, *, lam_init):
    x = x_ref[...]
    d = x.shape[1]
    h = _rms(x, g_ref[...]).astype(BF16)
    lp = lam_ref[...]
    lam = (jnp.exp(jnp.sum(lp[0:1] * lp[1:2], axis=1, keepdims=True))
           - jnp.exp(jnp.sum(lp[2:3] * lp[3:4], axis=1, keepdims=True)) + lam_init)
    oc = oc_ref[...]
    parts = []
    for hh in range(DIFF_HEADS):
        p1 = oc[:, (2 * hh) * LANES:(2 * hh + 1) * LANES]
        p2 = oc[:, (2 * hh + 1) * LANES:(2 * hh + 2) * LANES]
        parts.append((_rms(p1 - lam * p2, subln_ref[...]) * (1.0 - lam_init)).astype(BF16))
    ocn = jnp.concatenate(parts, axis=1)

    def gate(i):
        return jax.nn.sigmoid(_dot(h, wg_ref[:, i * d:(i + 1) * d]))

    merged = gate(0) * _dot(oa_ref[...], wba_ref[...])
    merged += gate(1) * _dot(ob_ref[...], wbb_ref[...])
    merged += gate(2) * _dot(ocn, wbc_ref[...])
    o_ref[...] = x + _dot(merged.astype(BF16), wo_ref[...])


def merge(x, g, wg, oa, ob, oc, lam_p, subln, wba, wbb, wbc, wo, *, lam_init, tm):
    n, d = x.shape
    full = lambda a: pl.BlockSpec(a.shape, lambda i: (0,) * a.ndim)
    row = lambda a: pl.BlockSpec((tm, a.shape[1]), lambda i: (i, 0))
    g2 = g.reshape(1, d)
    subln2 = subln.reshape(1, -1)
    return pl.pallas_call(
        functools.partial(_merge_kernel, lam_init=lam_init),
        out_shape=jax.ShapeDtypeStruct((n, d), F32),
        grid=(n // tm,),
        in_specs=[row(x), full(g2), full(wg), row(oa), row(ob), row(oc), full(lam_p), full(subln2),
                  full(wba), full(wbb), full(wbc), full(wo)],
        out_specs=row(x),
        compiler_params=_params("parallel"),
        name="merge",
    )(x, g2, wg, oa, ob, oc, lam_p, subln2, wba, wbb, wbc, wo)


def _cross_kernel(x_ref, g_ref, wq_ref, k_ref, v_ref, wo_ref, o_ref, *, scale):
    x = x_ref[...]
    h = _rms(x, g_ref[...]).astype(BF16)
    q = (_dot(h, wq_ref[...]) * scale).astype(BF16)
    outs = []
    for hh in range(X_HEADS):
        sl = slice(hh * X_DIM, (hh + 1) * X_DIM)
        s = _dot_nt(q[:, sl], k_ref[:, sl])
        p = jnp.exp(s - jnp.max(s, axis=1, keepdims=True))
        o = _dot(p.astype(BF16), v_ref[:, sl]) / jnp.sum(p, axis=1, keepdims=True)
        outs.append(o.astype(BF16))
    o_ref[...] = x + _dot(jnp.concatenate(outs, axis=1), wo_ref[...])


def cross_attention(x, g, wq, kv, wo, *, batch, seq, tm):
    n, d = x.shape
    n_mem = kv.shape[0] // batch
    width = X_HEADS * X_DIM
    steps = seq // tm
    full = lambda a: pl.BlockSpec(a.shape, lambda b, i: (0,) * a.ndim)
    g2 = g.reshape(1, d)
    return pl.pallas_call(
        functools.partial(_cross_kernel, scale=X_DIM ** -0.5),
        out_shape=jax.ShapeDtypeStruct((n, d), F32),
        grid=(batch, steps),
        in_specs=[
            pl.BlockSpec((tm, d), lambda b, i: (b * steps + i, 0)),
            full(g2), full(wq),
            pl.BlockSpec((n_mem, width), lambda b, i: (b, 0)),
            pl.BlockSpec((n_mem, width), lambda b, i: (b, 1)),
            full(wo),
        ],
        out_specs=pl.BlockSpec((tm, d), lambda b, i: (b * steps + i, 0)),
        compiler_params=_params("parallel", "parallel"),
        name="cross_attention",
    )(x, g2, wq, kv, kv, wo)


def _router_kernel(x_ref, g_ref, w_ref, b_ref, comb_ref):
    h = _rms(x_ref[...], g_ref[...])
    logits = jnp.dot(h, w_ref[...], preferred_element_type=F32, precision=lax.Precision.HIGHEST) + b_ref[...]
    lane = lax.broadcasted_iota(jnp.int32, logits.shape, 1).astype(F32)
    neg = -jnp.inf
    far = float(4 * LANES)

    def first_argmax(vals, vmax):
        return jnp.min(jnp.where(vals == vmax, lane, far), axis=1, keepdims=True)

    gl = jnp.where((lane >= GROUP_LANE0) & (lane < GROUP_LANE0 + N_GROUPS), logits, neg)
    gmax = jnp.max(gl, axis=1, keepdims=True)
    lo = (first_argmax(gl, gmax) - GROUP_LANE0) * EXPERTS_PER_GROUP
    p_g = 1.0 / jnp.sum(jnp.exp(gl - gmax), axis=1, keepdims=True)
    el = jnp.where((lane >= lo) & (lane < lo + EXPERTS_PER_GROUP), logits, neg)
    v1 = jnp.max(el, axis=1, keepdims=True)
    i1 = first_argmax(el, v1)
    el2 = jnp.where(lane == i1, neg, el)
    v2 = jnp.max(el2, axis=1, keepdims=True)
    i2 = first_argmax(el2, v2)
    t = jnp.exp(v2 - v1)
    w1 = 1.0 / (1.0 + t)
    comb_ref[...] = jnp.where(lane == i1, w1 * p_g, 0.0) + jnp.where(lane == i2, (t * w1) * p_g, 0.0)


def router(x, g, w, b, *, tm):
    n, d = x.shape
    full = lambda a: pl.BlockSpec(a.shape, lambda i: (0,) * a.ndim)
    g2 = g.reshape(1, d)
    return pl.pallas_call(
        _router_kernel,
        out_shape=jax.ShapeDtypeStruct((n, LANES), F32),
        grid=(n // tm,),
        in_specs=[pl.BlockSpec((tm, d), lambda i: (i, 0)), full(g2), full(w), full(b)],
        out_specs=pl.BlockSpec((tm, LANES), lambda i: (i, 0)),
        compiler_params=_params("parallel"),
        name="router",
    )(x, g2, w, b)


def _moe_kernel(x_ref, g_ref, comb_ref, wg_ref, wu_ref, wd_ref, fg_ref, o_ref, h_scr, y_scr, *, ec, final_norm):
    j = pl.program_id(1)

    @pl.when(j == 0)
    def _():
        h_scr[...] = _rms(x_ref[...], g_ref[...]).astype(BF16)
        y_scr[...] = jnp.zeros(y_scr.shape, F32)

    h = h_scr[...]
    comb = comb_ref[...]
    lane = lax.broadcasted_iota(jnp.int32, comb.shape, 1)
    for e in range(ec):
        a = _dot(h, wg_ref[e])
        u = _dot(h, wu_ref[e])
        cw = jnp.sum(jnp.where(lane == j * ec + e, comb, 0.0), axis=1, keepdims=True)
        act = (a * jax.nn.sigmoid(a)) * u * cw
        y_scr[...] += _dot(act.astype(BF16), wd_ref[e])

    @pl.when(j == pl.num_programs(1) - 1)
    def _():
        y = x_ref[...] + y_scr[...]
        if final_norm:
            y = _rms(y, fg_ref[...])
        o_ref[...] = y


def moe(x, g, comb, wg, wu, wd, fg, *, tm, ec, final_norm):
    n, d = x.shape
    n_exp, _, f = wg.shape
    g2 = g.reshape(1, d)
    fg2 = fg.reshape(1, d)
    return pl.pallas_call(
        functools.partial(_moe_kernel, ec=ec, final_norm=final_norm),
        out_shape=jax.ShapeDtypeStruct((n, d), F32),
        grid=(n // tm, n_exp // ec),
        in_specs=[
            pl.BlockSpec((tm, d), lambda i, j: (i, 0)),
            pl.BlockSpec((1, d), lambda i, j: (0, 0)),
            pl.BlockSpec((tm, LANES), lambda i, j: (i, 0)),
            pl.BlockSpec((ec, d, f), lambda i, j: (j, 0, 0)),
            pl.BlockSpec((ec, d, f), lambda i, j: (j, 0, 0)),
            pl.BlockSpec((ec, f, d), lambda i, j: (j, 0, 0)),
            pl.BlockSpec((1, d), lambda i, j: (0, 0)),
        ],
        out_specs=pl.BlockSpec((tm, d), lambda i, j: (i, 0)),
        scratch_shapes=[pltpu.VMEM((tm, d), BF16), pltpu.VMEM((tm, d), F32)],
        compiler_params=_params("parallel", "arbitrary"),
        name="moe",
    )(x, g2, comb, wg, wu, wd, fg2)


def _pad_heads(w, heads, dim, lane0=0):
    k = w.shape[0]
    w = w.reshape(k, heads, dim)
    w = jnp.pad(w, ((0, 0), (0, 0), (lane0, LANES - dim - lane0)))
    return w.reshape(k, heads * LANES)


def _pad_head_rows(w, heads, dim):
    m = w.shape[1]
    w = w.reshape(heads, dim, m)
    w = jnp.pad(w, ((0, 0), (0, LANES - dim), (0, 0)))
    return w.reshape(heads * LANES, m)


def _swap_halves(w):
    half = w.shape[-1] // 2
    return jnp.concatenate([w[..., half:], w[..., :half]], axis=-1)


def _rope_tables(seq):
    half = MLA_ROPE // 2
    inv_freq = ROPE_THETA ** (-jnp.arange(half, dtype=F32) / half)
    ang = jnp.arange(seq, dtype=F32)[:, None] * inv_freq[None, :]
    cos, sin = jnp.cos(ang), jnp.sin(ang)
    ones = jnp.ones((seq, MLA_NOPE), F32)
    zn = jnp.zeros((seq, MLA_NOPE), F32)
    zp = jnp.zeros((seq, LANES - MLA_NOPE - MLA_ROPE), F32)
    return (jnp.concatenate([ones, cos, cos, zp], axis=1),
            jnp.concatenate([zn, -sin, sin, zp], axis=1))


def kernel(x, mem, attn_norm, w_in, q_norm, w_uq, kv_norm, w_ukv, diff_lambda, diff_subln, w_branch, w_out,
           cross_norm, mem_norm, w_xq, w_xkv, w_xo, ffn_norm, w_router_grp, b_router_grp, w_router_exp,
           b_router_exp, w_exp_gate, w_exp_up, w_exp_down, final_norm):
    batch, seq, d = x.shape
    depth = w_in.shape[0]
    n = batch * seq
    n_mem = mem.shape[1]
    tm = min(512, seq)
    tm_big = min(1024, seq)
    tq_attn = min(1024, seq)
    tk_attn = tq_attn // 2
    ts_attn = min(64, seq)
    ts_sb = min(256, seq)

    gate_cols = 3 * d
    o_cq = gate_cols
    o_ckv = o_cq + MLA_Q_RANK
    o_kr = o_ckv + MLA_KV_RANK
    o_sb = o_kr + MLA_ROPE
    sb_w = SB_HEADS * SB_DIM
    o_diff = o_sb + 3 * sb_w
    diff_qk = DIFF_HEADS * 2 * DIFF_DIM

    cos_t, sin_t = _rope_tables(seq)
    xf = x.reshape(n, d)
    mem_f = mem.reshape(batch * n_mem, d)

    for l in range(depth):
        wl = w_in[l]
        kr_w = wl[:, o_kr:o_sb]
        w1 = jnp.concatenate([
            wl[:, o_cq:o_kr],
            jnp.pad(kr_w, ((0, 0), (MLA_NOPE, LANES - MLA_NOPE - MLA_ROPE))),
            jnp.pad(_swap_halves(kr_w), ((0, 0), (MLA_NOPE, LANES - MLA_NOPE - MLA_ROPE))),
        ], axis=1).astype(BF16)
        uq = w_uq[l].reshape(MLA_Q_RANK, MLA_HEADS, MLA_NOPE + MLA_ROPE)
        uq_nope, uq_rope = uq[..., :MLA_NOPE], uq[..., MLA_NOPE:]
        padq = ((0, 0), (0, 0), (0, LANES - MLA_NOPE - MLA_ROPE))
        wq = jnp.pad(jnp.concatenate([uq_nope, uq_rope], -1), padq).reshape(MLA_Q_RANK, -1).astype(BF16)
        wqs = jnp.pad(jnp.concatenate([jnp.zeros_like(uq_nope), _swap_halves(uq_rope)], -1), padq)
        wqs = wqs.reshape(MLA_Q_RANK, -1).astype(BF16)
        ukv = w_ukv[l].reshape(MLA_KV_RANK, MLA_HEADS, MLA_NOPE + MLA_V)
        wk = _pad_heads(ukv[..., :MLA_NOPE].reshape(MLA_KV_RANK, -1), MLA_HEADS, MLA_NOPE).astype(BF16)
        wv = _pad_heads(ukv[..., MLA_NOPE:].reshape(MLA_KV_RANK, -1), MLA_HEADS, MLA_V).astype(BF16)
        q_a, k_a, v_a = mla_prep(xf, attn_norm[l], w1, q_norm[l], kv_norm[l], wq, wqs, wk, wv,
                                 cos_t, sin_t, tm=tm, seq=seq)

        sb = wl[:, o_sb:o_diff]
        dq = wl[:, o_diff:o_diff + diff_qk].reshape(d, DIFF_HEADS, 2, DIFF_DIM)
        dq = jnp.stack([jnp.pad(dq[:, :, 0], ((0, 0), (0, 0), (0, DIFF_DIM))),
                        jnp.pad(dq[:, :, 1], ((0, 0), (0, 0), (DIFF_DIM, 0)))], axis=2)
        w2 = jnp.concatenate([
            _pad_heads(sb[:, :sb_w], SB_HEADS, SB_DIM),
            _pad_heads(sb[:, sb_w:2 * sb_w], SB_HEADS, SB_DIM),
            _pad_heads(sb[:, 2 * sb_w:], SB_HEADS, SB_DIM),
            dq.reshape(d, 2 * DIFF_HEADS * LANES),
            wl[:, o_diff + diff_qk:o_diff + 3 * diff_qk],
        ], axis=1).astype(BF16)
        p2 = norm_matmul(xf, attn_norm[l], w2, tm=tm_big, tn=w2.shape[1] // 2)
        cb_sb_q, cb_sb_k, cb_sb_v = 0, SB_HEADS, 2 * SB_HEADS
        cb_d_q = 3 * SB_HEADS
        cb_d_k = cb_d_q + 2 * DIFF_HEADS
        cb_d_v = cb_d_k + DIFF_HEADS

        o_a = flash_attention(q_a, k_a, v_a, batch=batch, seq=seq, heads=MLA_HEADS, q_col0=0, k_col0=0,
                              v_col0=0, heads_per_kv=1, tq=tq_attn, tk=tk_attn, ts=ts_attn, unroll=2)
        o_b = sb_attention(p2, p2, p2, batch=batch, seq=seq, heads=SB_HEADS, q_col0=cb_sb_q,
                           k_col0=cb_sb_k, v_col0=cb_sb_v, tq=tq_attn, tk=tk_attn, tw=min(256, tk_attn), ts=ts_sb,
                           q_scale=SB_DIM ** -0.5 * LOG2E)
        slopes = tuple(2.0 ** (-8.0 * (i + 1) / DIFF_HEADS) for i in range(DIFF_HEADS))
        o_c = flash_attention(p2, p2, p2, batch=batch, seq=seq, heads=2 * DIFF_HEADS, q_col0=cb_d_q,
                              k_col0=cb_d_k, v_col0=cb_d_v, heads_per_kv=2, tq=tq_attn, tk=tk_attn, ts=ts_attn,
                              unroll=2, q_scale=DIFF_DIM ** -0.5 * LOG2E, slopes=slopes, heads_per_slope=2, out_dtype=F32)

        lam_init = 0.8 - 0.6 * math.exp(-0.3 * l)
        wba = _pad_head_rows(w_branch[l, 0], MLA_HEADS, MLA_V).astype(BF16)
        wbb = _pad_head_rows(w_branch[l, 1], SB_HEADS, SB_DIM).astype(BF16)
        xf = merge(xf, attn_norm[l], wl[:, :gate_cols].astype(BF16), o_a, o_b, o_c, diff_lambda[l],
                   diff_subln[l], wba, wbb, w_branch[l, 2].astype(BF16), w_out[l].astype(BF16),
                   lam_init=lam_init, tm=min(256, seq))

        kv = norm_matmul(mem_f, mem_norm, w_xkv[l].astype(BF16), tm=min(256, n_mem), tn=512)
        xf = cross_attention(xf, cross_norm[l], w_xq[l].astype(BF16), kv, w_xo[l].astype(BF16),
                             batch=batch, seq=seq, tm=tm)

        pad_r = LANES - N_EXPERTS - N_GROUPS
        wr = jnp.pad(jnp.concatenate([w_router_exp[l], w_router_grp[l]], axis=1), ((0, 0), (0, pad_r)))
        br = jnp.pad(jnp.concatenate([b_router_exp[l], b_router_grp[l]]), (0, pad_r)).reshape(1, LANES)
        comb = router(xf, ffn_norm[l], wr, br, tm=tm)
        xf = moe(xf, ffn_norm[l], comb, w_exp_gate[l].astype(BF16), w_exp_up[l].astype(BF16),
                 w_exp_down[l].astype(BF16), final_norm, tm=tm_big, ec=EXPERTS_PER_GROUP,
                 final_norm=(l == depth - 1))

    return xf.reshape(batch, seq, d)
```

```python
import functools
import math

import jax
import jax.numpy as jnp
from jax import lax
from jax.experimental import pallas as pl
from jax.experimental.pallas import tpu as pltpu

F32 = jnp.float32
BF16 = jnp.bfloat16
EPS = 1e-6
LANES = 128
LOG2E = math.log2(math.e)
VMEM_LIMIT_BYTES = 56 * 1024 * 1024

ROPE_THETA = 10000.0
MLA_HEADS, MLA_NOPE, MLA_ROPE, MLA_V = 8, 64, 32, 64
MLA_Q_RANK, MLA_KV_RANK = 256, 256
SB_HEADS, SB_DIM = 8, 64
DIFF_HEADS, DIFF_DIM = 4, 64
X_HEADS, X_DIM = 4, 128
N_GROUPS, EXPERTS_PER_GROUP = 4, 8
N_EXPERTS = N_GROUPS * EXPERTS_PER_GROUP
GROUP_LANE0 = N_EXPERTS


def _params(*semantics):
    return pltpu.CompilerParams(dimension_semantics=semantics, vmem_limit_bytes=VMEM_LIMIT_BYTES)


def _rms(x, g):
    return x * lax.rsqrt(jnp.mean(x * x, axis=-1, keepdims=True) + EPS) * g


def _dot(a, b):
    return jnp.dot(a, b, preferred_element_type=F32)


def _dot_nt(a, b):
    return lax.dot_general(a, b, (((1,), (1,)), ((), ())), preferred_element_type=F32)


def _norm_matmul_kernel(x_ref, g_ref, w_ref, o_ref, h_scr):
    @pl.when(pl.program_id(1) == 0)
    def _():
        h_scr[...] = _rms(x_ref[...], g_ref[...]).astype(BF16)

    o_ref[...] = _dot(h_scr[...], w_ref[...]).astype(o_ref.dtype)


def norm_matmul(x, g, w, *, tm, tn, out_dtype=BF16):
    n, k = x.shape
    m = w.shape[1]
    return pl.pallas_call(
        _norm_matmul_kernel,
        out_shape=jax.ShapeDtypeStruct((n, m), out_dtype),
        grid=(n // tm, m // tn),
        in_specs=[
            pl.BlockSpec((tm, k), lambda i, j: (i, 0)),
            pl.BlockSpec((1, k), lambda i, j: (0, 0)),
            pl.BlockSpec((k, tn), lambda i, j: (0, j)),
        ],
        out_specs=pl.BlockSpec((tm, tn), lambda i, j: (i, j)),
        scratch_shapes=[pltpu.VMEM((tm, k), BF16)],
        compiler_params=_params("parallel", "arbitrary"),
        name="norm_matmul",
    )(x, g.reshape(1, k), w)


def _mla_prep_kernel(x_ref, g_ref, w1_ref, qn_ref, kvn_ref, wq_ref, wqs_ref, wk_ref, wv_ref,
                     cos_ref, sin_ref, q_ref, k_ref, v_ref, *, scale):
    h = _rms(x_ref[...], g_ref[...]).astype(BF16)
    c = _dot(h, w1_ref[...])
    cq = _rms(c[:, :MLA_Q_RANK], qn_ref[...]).astype(BF16)
    ckv = _rms(c[:, MLA_Q_RANK:MLA_Q_RANK + MLA_KV_RANK], kvn_ref[...]).astype(BF16)
    base = MLA_Q_RANK + MLA_KV_RANK
    cos = cos_ref[...]
    sin = sin_ref[...]
    kr = c[:, base:base + LANES] * cos + c[:, base + LANES:base + 2 * LANES] * sin
    q = _dot(cq, wq_ref[...])
    qs = _dot(cq, wqs_ref[...])
    k = _dot(ckv, wk_ref[...])
    v_ref[...] = _dot(ckv, wv_ref[...]).astype(BF16)
    for hh in range(MLA_HEADS):
        sl = slice(hh * LANES, (hh + 1) * LANES)
        q_ref[:, sl] = ((q[:, sl] * cos + qs[:, sl] * sin) * scale).astype(BF16)
        k_ref[:, sl] = (k[:, sl] + kr).astype(BF16)


def mla_prep(x, g, w1, qn, kvn, wq, wqs, wk, wv, cos_t, sin_t, *, tm, seq):
    n, d = x.shape
    width = MLA_HEADS * LANES
    steps_per_seq = seq // tm
    full = lambda a: pl.BlockSpec(a.shape, lambda i: (0,) * a.ndim)
    out = jax.ShapeDtypeStruct((n, width), BF16)
    row = pl.BlockSpec((tm, width), lambda i: (i, 0))
    args = (g.reshape(1, d), w1, qn.reshape(1, -1), kvn.reshape(1, -1), wq, wqs, wk, wv)
    return pl.pallas_call(
        functools.partial(_mla_prep_kernel, scale=(MLA_NOPE + MLA_ROPE) ** -0.5 * LOG2E),
        out_shape=(out, out, out),
        grid=(n // tm,),
        in_specs=[pl.BlockSpec((tm, d), lambda i: (i, 0))] + [full(a) for a in args] + [
            pl.BlockSpec((tm, LANES), lambda i: (i % steps_per_seq, 0)),
            pl.BlockSpec((tm, LANES), lambda i: (i % steps_per_seq, 0)),
        ],
        out_specs=(row, row, row),
        compiler_params=_params("parallel"),
        name="mla_prep",
    )(x, *args, cos_t, sin_t)


def _flash_kernel(q_ref, k_ref, v_ref, o_ref, s_scr, p_scr, alpha_scr, m_scr, l_scr, acc_scr, *, tq, tk, ts,
                  q_scale, slopes, heads_per_slope):
    qi = pl.program_id(2)
    nchunk = tk // LANES
    n_full = 2 * qi
    m_scr[...] = jnp.full(m_scr.shape, -jnp.inf, F32)
    l_scr[...] = jnp.zeros(l_scr.shape, F32)
    acc_scr[...] = jnp.zeros(acc_scr.shape, F32)
    q = q_ref[...]
    if q_scale != 1.0:
        q = (q.astype(F32) * q_scale).astype(BF16)

    if slopes is not None:
        slope_idx = pl.program_id(1) // heads_per_slope
        slope = jnp.float32(slopes[-1])
        for i in range(len(slopes) - 2, -1, -1):
            slope = jnp.where(slope_idx == i, jnp.float32(slopes[i]), slope)
        col = lax.broadcasted_iota(jnp.int32, (1, tk), 1)
    lane_minus_row = (lax.broadcasted_iota(jnp.int32, (ts, LANES), 1)
                      - lax.broadcasted_iota(jnp.int32, (ts, LANES), 0))

    def scores(kb, slot):
        s = _dot_nt(q, k_ref[pl.ds(pl.multiple_of(kb * tk, tk), tk), :])
        if slopes is not None:
            s = s + (slope * LOG2E) * (col + (kb * tk - qi * tq)).astype(F32)
        s_scr[slot] = s

    def weighted_values(kb, slot):
        pv = _dot(p_scr[slot], v_ref[pl.ds(pl.multiple_of(kb * tk, tk), tk), :])
        acc_scr[...] = alpha_scr[slot] * acc_scr[...] + pv

    def softmax(slot, local):
        for u in range(tq // ts):
            rows = slice(u * ts, (u + 1) * ts)
            last = u * ts + ts - 1
            live = [c for c in range(nchunk) if local is None or local + c * LANES <= last]
            if not live:
                p_scr[slot, rows, :] = jnp.zeros((ts, tk), BF16)
                alpha_scr[slot, rows, :] = jnp.ones((ts, LANES), F32)
                continue
            sc = [s_scr[slot, rows, c * LANES:(c + 1) * LANES] for c in live]
            if local is not None:
                for i, c in enumerate(live):
                    if local + c * LANES + LANES - 1 > u * ts:
                        sc[i] = jnp.where(lane_minus_row <= u * ts - local - c * LANES, sc[i], -jnp.inf)
            m_prev = m_scr[rows, :]
            m_new = jnp.maximum(m_prev, jnp.max(functools.reduce(jnp.maximum, sc), axis=1, keepdims=True))
            alpha = jnp.exp2(m_prev - m_new)
            ps = [jnp.exp2(x - m_new) for x in sc]
            l_scr[rows, :] = alpha * l_scr[rows, :] + jnp.sum(functools.reduce(jnp.add, ps), axis=1, keepdims=True)
            for c, pc in zip(live, ps):
                p_scr[slot, rows, c * LANES:(c + 1) * LANES] = pc.astype(BF16)
            if len(live) < nchunk:
                p_scr[slot, rows, len(live) * LANES:] = jnp.zeros((ts, tk - len(live) * LANES), BF16)
            alpha_scr[slot, rows, :] = alpha
            m_scr[rows, :] = m_new

    last_full = jnp.maximum(n_full - 1, 0)
    scores(n_full, 0)
    scores(n_full + 1, 1)
    softmax(0, 0)
    weighted_values(n_full, 0)
    softmax(1, tk)
    scores(0, 0)

    def body(i, carry):
        weighted_values(jnp.where(i == 0, n_full + 1, 2 * i - 1), 1)
        softmax(0, None)
        scores(jnp.minimum(2 * i + 1, last_full), 1)
        weighted_values(2 * i, 0)
        softmax(1, None)
        scores(jnp.minimum(2 * i + 2, last_full), 0)
        return carry

    lax.fori_loop(0, qi, body, 0)
    weighted_values(jnp.where(qi == 0, 1, n_full - 1), 1)
    o_ref[...] = (acc_scr[...] / l_scr[...]).astype(o_ref.dtype)


def flash_attention(q, k, v, *, batch, seq, heads, q_col0, k_col0, v_col0, heads_per_kv, tq, tk, ts,
                    q_scale=1.0, slopes=None, heads_per_slope=1, out_dtype=BF16):
    nq = seq // tq
    assert tq == 2 * tk
    kernel = functools.partial(_flash_kernel, tq=tq, tk=tk, ts=ts, q_scale=q_scale, slopes=slopes,
                               heads_per_slope=heads_per_slope)
    return pl.pallas_call(
        kernel,
        out_shape=jax.ShapeDtypeStruct((batch * seq, heads * LANES), out_dtype),
        grid=(batch, heads, nq),
        in_specs=[
            pl.BlockSpec((tq, LANES), lambda b, h, i: (b * nq + i, q_col0 + h)),
            pl.BlockSpec((seq, LANES), lambda b, h, i: (b, k_col0 + h // heads_per_kv)),
            pl.BlockSpec((seq, LANES), lambda b, h, i: (b, v_col0 + h // heads_per_kv)),
        ],
        out_specs=pl.BlockSpec((tq, LANES), lambda b, h, i: (b * nq + i, h)),
        scratch_shapes=[pltpu.VMEM((2, tq, tk), F32), pltpu.VMEM((2, tq, tk), BF16),
                        pltpu.VMEM((2, tq, LANES), F32)] + [pltpu.VMEM((tq, LANES), F32)] * 3,
        compiler_params=_params("parallel", "parallel", "arbitrary"),
        name="flash_attention",
    )(q, k, v)


def _sb_kernel(q_ref, k_ref, v_ref, tri_ref, o_ref, z_scr, p_scr, c_scr, acc_scr, *, tq, tk, tw, ts, q_scale):
    qi = pl.program_id(2)
    ratio = tq // tk
    nchunk = tk // LANES
    q = (q_ref[...].astype(F32) * q_scale).astype(BF16)
    tri = tri_ref[...]
    c_scr[...] = jnp.zeros(c_scr.shape, F32)
    acc_scr[...] = jnp.zeros(acc_scr.shape, F32)
    p_scr[1] = jnp.zeros(p_scr.shape[1:], BF16)
    lane_minus_row = (lax.broadcasted_iota(jnp.int32, (ts, LANES), 1)
                      - lax.broadcasted_iota(jnp.int32, (ts, LANES), 0))

    def scores(kc, slot):
        z_scr[slot] = _dot_nt(q, k_ref[pl.ds(pl.multiple_of(kc * tk, tk), tk), :])

    def weighted_values(kc, slot):
        acc_scr[...] += _dot(p_scr[slot], v_ref[pl.ds(pl.multiple_of(kc * tk, tk), tk), :])

    def weights(slot, local):
        for u in range(tq // ts):
            rows = slice(u * ts, (u + 1) * ts)
            def chunk_state(c):
                if local is None or local + c * LANES + LANES - 1 < u * ts:
                    return "full"
                return "dead" if local + c * LANES >= u * ts + ts - 1 else "masked"

            state = [chunk_state(c) for c in range(nchunk)]
            lss, lks, valid = {}, {}, {}
            for c in range(nchunk):
                if state[c] == "dead":
                    continue
                z = z_scr[slot, rows, c * LANES:(c + 1) * LANES]
                lss[c] = jnp.minimum(z, 0.0) - jnp.log2(1.0 + jnp.exp2(-jnp.abs(z)))
                lks[c] = lss[c] - z
                if state[c] == "masked":
                    valid[c] = lane_minus_row < u * ts - local - c * LANES
                    lks[c] = jnp.where(valid[c], lks[c], 0.0)
            later = c_scr[rows, :]
            gc = tw // LANES
            for g in range(nchunk // gc - 1, -1, -1):
                chunks = range(g * gc, (g + 1) * gc)
                live = [c for c in chunks if state[c] != "dead"]
                for c in chunks:
                    if state[c] == "dead":
                        p_scr[slot, rows, c * LANES:(c + 1) * LANES] = jnp.zeros((ts, LANES), BF16)
                if not live:
                    continue
                grp = [lks[c] if c in lks else jnp.zeros((ts, LANES), F32) for c in chunks]
                after = jnp.concatenate([later] * gc, axis=1) + _dot(jnp.concatenate(grp, axis=1).astype(BF16), tri)
                for c in live:
                    i = c - g * gc
                    p = jnp.exp2(lss[c] + after[:, i * LANES:(i + 1) * LANES])
                    if state[c] == "masked":
                        p = jnp.where(valid[c], p, 0.0)
                    p_scr[slot, rows, c * LANES:(c + 1) * LANES] = p.astype(BF16)
                later = later + jnp.sum(functools.reduce(jnp.add, [lks[c] for c in live]), axis=1, keepdims=True)
            if lks:
                c_scr[rows, :] = later

    n = (qi + 1) * ratio
    scores(n - 1, 0)
    for j in range(ratio):
        weighted_values(n - max(j, 1), 1 - j % 2)
        weights(j % 2, (ratio - 1 - j) * tk)
        scores(jnp.maximum(n - 2 - j, 0), 1 - j % 2)

    def trip(kc, slot):
        weighted_values(kc + 1, 1 - slot)
        weights(slot, None)
        scores(jnp.maximum(kc - 1, 0), 1 - slot)

    def body(i, carry):
        kc = qi * ratio - 1 - 2 * i
        trip(kc, 0)
        trip(kc - 1, 1)
        return carry

    lax.fori_loop(0, qi, body, 0)
    weighted_values(0, 1)
    o_ref[...] = acc_scr[...].astype(o_ref.dtype)


def sb_attention(q, k, v, *, batch, seq, heads, q_col0, k_col0, v_col0, tq, tk, tw, ts, q_scale):
    assert tq == 2 * tk and tk % tw == 0
    nq = seq // tq
    r = lax.broadcasted_iota(jnp.int32, (tw, tw), 0)
    c = lax.broadcasted_iota(jnp.int32, (tw, tw), 1)
    tri = (r > c).astype(BF16)
    return pl.pallas_call(
        functools.partial(_sb_kernel, tq=tq, tk=tk, tw=tw, ts=ts, q_scale=q_scale),
        out_shape=jax.ShapeDtypeStruct((batch * seq, heads * LANES), BF16),
        grid=(batch, heads, nq),
        in_specs=[
            pl.BlockSpec((tq, LANES), lambda b, h, i: (b * nq + i, q_col0 + h)),
            pl.BlockSpec((seq, LANES), lambda b, h, i: (b, k_col0 + h)),
            pl.BlockSpec((seq, LANES), lambda b, h, i: (b, v_col0 + h)),
            pl.BlockSpec((tw, tw), lambda b, h, i: (0, 0)),
        ],
        out_specs=pl.BlockSpec((tq, LANES), lambda b, h, i: (b * nq + i, h)),
        scratch_shapes=[pltpu.VMEM((2, tq, tk), F32), pltpu.VMEM((2, tq, tk), BF16),
                        pltpu.VMEM((tq, LANES), F32), pltpu.VMEM((tq, LANES), F32)],
        compiler_params=_params("parallel", "parallel", "arbitrary"),
        name="sb_attention",
    )(q, k, v, tri)


def _merge_kernel(x_ref, g_ref, wg_ref, oa_ref, ob_ref, oc_ref, lam_ref, subln_ref,
                  wba_ref, wbb_ref, wbc_ref, wo_ref, o_ref, *, lam_init):
    x = x_ref[...]
    d = x.shape[1]
    h = _rms(x, g_ref[...]).astype(BF16)
    lp = lam_ref[...]
    lam = (jnp.exp(jnp.sum(lp[0:1] * lp[1:2], axis=1, keepdims=True))
           - jnp.exp(jnp.sum(lp[2:3] * lp[3:4], axis=1, keepdims=True)) + lam_init)
    oc = oc_ref[...]
    parts = []
    for hh in range(DIFF_HEADS):
        p1 = oc[:, (2 * hh) * LANES:(2 * hh + 1) * LANES]
        p2 = oc[:, (2 * hh + 1) * LANES:(2 * hh + 2) * LANES]
        parts.append((_rms(p1 - lam * p2, subln_ref[...]) * (1.0 - lam_init)).astype(BF16))
    ocn = jnp.concatenate(parts, axis=1)

    def gate(i):
        return jax.nn.sigmoid(_dot(h, wg_ref[:, i * d:(i + 1) * d]))

    merged = gate(0) * _dot(oa_ref[...], wba_ref[...])
    merged += gate(1) * _dot(ob_ref[...], wbb_ref[...])
    merged += gate(2) * _dot(ocn, wbc_ref[...])
    o_ref[...] = x + _dot(merged.astype(BF16), wo_ref[...])


def merge(x, g, wg, oa, ob, oc, lam_p, subln, wba, wbb, wbc, wo, *, lam_init, tm):
    n, d = x.shape
    full = lambda a: pl.BlockSpec(a.shape, lambda i: (0,) * a.ndim)
    row = lambda a: pl.BlockSpec((tm, a.shape[1]), lambda i: (i, 0))
    g2 = g.reshape(1, d)
    subln2 = subln.reshape(1, -1)
    return pl.pallas_call(
        functools.partial(_merge_kernel, lam_init=lam_init),
        out_shape=jax.ShapeDtypeStruct((n, d), F32),
        grid=(n // tm,),
        in_specs=[row(x), full(g2), full(wg), row(oa), row(ob), row(oc), full(lam_p), full(subln2),
                  full(wba), full(wbb), full(wbc), full(wo)],
        out_specs=row(x),
        compiler_params=_params("parallel"),
        name="merge",
    )(x, g2, wg, oa, ob, oc, lam_p, subln2, wba, wbb, wbc, wo)


def _cross_kernel(x_ref, g_ref, wq_ref, k_ref, v_ref, wo_ref, o_ref, *, scale):
    x = x_ref[...]
    h = _rms(x, g_ref[...]).astype(BF16)
    q = (_dot(h, wq_ref[...]) * scale).astype(BF16)
    outs = []
    for hh in range(X_HEADS):
        sl = slice(hh * X_DIM, (hh + 1) * X_DIM)
        s = _dot_nt(q[:, sl], k_ref[:, sl])
        p = jnp.exp(s - jnp.max(s, axis=1, keepdims=True))
        o = _dot(p.astype(BF16), v_ref[:, sl]) / jnp.sum(p, axis=1, keepdims=True)
        outs.append(o.astype(BF16))
    o_ref[...] = x + _dot(jnp.concatenate(outs, axis=1), wo_ref[...])


def cross_attention(x, g, wq, kv, wo, *, batch, seq, tm):
    n, d = x.shape
    n_mem = kv.shape[0] // batch
    width = X_HEADS * X_DIM
    steps = seq // tm
    full = lambda a: pl.BlockSpec(a.shape, lambda b, i: (0,) * a.ndim)
    g2 = g.reshape(1, d)
    return pl.pallas_call(
        functools.partial(_cross_kernel, scale=X_DIM ** -0.5),
        out_shape=jax.ShapeDtypeStruct((n, d), F32),
        grid=(batch, steps),
        in_specs=[
            pl.BlockSpec((tm, d), lambda b, i: (b * steps + i, 0)),
            full(g2), full(wq),
            pl.BlockSpec((n_mem, width), lambda b, i: (b, 0)),
            pl.BlockSpec((n_mem, width), lambda b, i: (b, 1)),
            full(wo),
        ],
        out_specs=pl.BlockSpec((tm, d), lambda b, i: (b * steps + i, 0)),
        compiler_params=_params("parallel", "parallel"),
        name="cross_attention",
    )(x, g2, wq, kv, kv, wo)


def _router_kernel(x_ref, g_ref, w_ref, b_ref, comb_ref):
    h = _rms(x_ref[...], g_ref[...])
    logits = jnp.dot(h, w_ref[...], preferred_element_type=F32, precision=lax.Precision.HIGHEST) + b_ref[...]
    lane = lax.broadcasted_iota(jnp.int32, logits.shape, 1).astype(F32)
    neg = -jnp.inf
    far = float(4 * LANES)

    def first_argmax(vals, vmax):
        return jnp.min(jnp.where(vals == vmax, lane, far), axis=1, keepdims=True)

    gl = jnp.where((lane >= GROUP_LANE0) & (lane < GROUP_LANE0 + N_GROUPS), logits, neg)
    gmax = jnp.max(gl, axis=1, keepdims=True)
    lo = (first_argmax(gl, gmax) - GROUP_LANE0) * EXPERTS_PER_GROUP
    p_g = 1.0 / jnp.sum(jnp.exp(gl - gmax), axis=1, keepdims=True)
    el = jnp.where((lane >= lo) & (lane < lo + EXPERTS_PER_GROUP), logits, neg)
    v1 = jnp.max(el, axis=1, keepdims=True)
    i1 = first_argmax(el, v1)
    el2 = jnp.where(lane == i1, neg, el)
    v2 = jnp.max(el2, axis=1, keepdims=True)
    i2 = first_argmax(el2, v2)
    t = jnp.exp(v2 - v1)
    w1 = 1.0 / (1.0 + t)
    comb_ref[...] = jnp.where(lane == i1, w1 * p_g, 0.0) + jnp.where(lane == i2, (t * w1) * p_g, 0.0)


def router(x, g, w, b, *, tm):
    n, d = x.shape
    full = lambda a: pl.BlockSpec(a.shape, lambda i: (0,) * a.ndim)
    g2 = g.reshape(1, d)
    return pl.pallas_call(
        _router_kernel,
        out_shape=jax.ShapeDtypeStruct((n, LANES), F32),
        grid=(n // tm,),
        in_specs=[pl.BlockSpec((tm, d), lambda i: (i, 0)), full(g2), full(w), full(b)],
        out_specs=pl.BlockSpec((tm, LANES), lambda i: (i, 0)),
        compiler_params=_params("parallel"),
        name="router",
    )(x, g2, w, b)


def _moe_kernel(x_ref, g_ref, comb_ref, wg_ref, wu_ref, wd_ref, fg_ref, o_ref, h_scr, y_scr, *, ec, final_norm):
    j = pl.program_id(1)

    @pl.when(j == 0)
    def _():
        h_scr[...] = _rms(x_ref[...], g_ref[...]).astype(BF16)
        y_scr[...] = jnp.zeros(y_scr.shape, F32)

    h = h_scr[...]
    comb = comb_ref[...]
    lane = lax.broadcasted_iota(jnp.int32, comb.shape, 1)
    for e in range(ec):
        a = _dot(h, wg_ref[e])
        u = _dot(h, wu_ref[e])
        cw = jnp.sum(jnp.where(lane == j * ec + e, comb, 0.0), axis=1, keepdims=True)
        act = (a * jax.nn.sigmoid(a)) * u * cw
        y_scr[...] += _dot(act.astype(BF16), wd_ref[e])

    @pl.when(j == pl.num_programs(1) - 1)
    def _():
        y = x_ref[...] + y_scr[...]
        if final_norm:
            y = _rms(y, fg_ref[...])
        o_ref[...] = y


def moe(x, g, comb, wg, wu, wd, fg, *, tm, ec, final_norm):
    n, d = x.shape
    n_exp, _, f = wg.shape
    g2 = g.reshape(1, d)
    fg2 = fg.reshape(1, d)
    return pl.pallas_call(
        functools.partial(_moe_kernel, ec=ec, final_norm=final_norm),
        out_shape=jax.ShapeDtypeStruct((n, d), F32),
        grid=(n // tm, n_exp // ec),
        in_specs=[
            pl.BlockSpec((tm, d), lambda i, j: (i, 0)),
            pl.BlockSpec((1, d), lambda i, j: (0, 0)),
            pl.BlockSpec((tm, LANES), lambda i, j: (i, 0)),
            pl.BlockSpec((ec, d, f), lambda i, j: (j, 0, 0)),
            pl.BlockSpec((ec, d, f), lambda i, j: (j, 0, 0)),
            pl.BlockSpec((ec, f, d), lambda i, j: (j, 0, 0)),
            pl.BlockSpec((1, d), lambda i, j: (0, 0)),
        ],
        out_specs=pl.BlockSpec((tm, d), lambda i, j: (i, 0)),
        scratch_shapes=[pltpu.VMEM((tm, d), BF16), pltpu.VMEM((tm, d), F32)],
        compiler_params=_params("parallel", "arbitrary"),
        name="moe",
    )(x, g2, comb, wg, wu, wd, fg2)


def _pad_heads(w, heads, dim, lane0=0):
    k = w.shape[0]
    w = w.reshape(k, heads, dim)
    w = jnp.pad(w, ((0, 0), (0, 0), (lane0, LANES - dim - lane0)))
    return w.reshape(k, heads * LANES)


def _pad_head_rows(w, heads, dim):
    m = w.shape[1]
    w = w.reshape(heads, dim, m)
    w = jnp.pad(w, ((0, 0), (0, LANES - dim), (0, 0)))
    return w.reshape(heads * LANES, m)


def _swap_halves(w):
    half = w.shape[-1] // 2
    return jnp.concatenate([w[..., half:], w[..., :half]], axis=-1)


def _rope_tables(seq):
    half = MLA_ROPE // 2
    inv_freq = ROPE_THETA ** (-jnp.arange(half, dtype=F32) / half)
    ang = jnp.arange(seq, dtype=F32)[:, None] * inv_freq[None, :]
    cos, sin = jnp.cos(ang), jnp.sin(ang)
    ones = jnp.ones((seq, MLA_NOPE), F32)
    zn = jnp.zeros((seq, MLA_NOPE), F32)
    zp = jnp.zeros((seq, LANES - MLA_NOPE - MLA_ROPE), F32)
    return (jnp.concatenate([ones, cos, cos, zp], axis=1),
            jnp.concatenate([zn, -sin, sin, zp], axis=1))


def kernel(x, mem, attn_norm, w_in, q_norm, w_uq, kv_norm, w_ukv, diff_lambda, diff_subln, w_branch, w_out,
           cross_norm, mem_norm, w_xq, w_xkv, w_xo, ffn_norm, w_router_grp, b_router_grp, w_router_exp,
           b_router_exp, w_exp_gate, w_exp_up, w_exp_down, final_norm):
    batch, seq, d = x.shape
    depth = w_in.shape[0]
    n = batch * seq
    n_mem = mem.shape[1]
    tm = min(512, seq)
    tm_big = min(1024, seq)
    tq_attn = min(1024, seq)
    tk_attn = tq_attn // 2
    ts_attn = min(64, seq)
    ts_sb = min(256, seq)

    gate_cols = 3 * d
    o_cq = gate_cols
    o_ckv = o_cq + MLA_Q_RANK
    o_kr = o_ckv + MLA_KV_RANK
    o_sb = o_kr + MLA_ROPE
    sb_w = SB_HEADS * SB_DIM
    o_diff = o_sb + 3 * sb_w
    diff_qk = DIFF_HEADS * 2 * DIFF_DIM

    cos_t, sin_t = _rope_tables(seq)
    xf = x.reshape(n, d)
    mem_f = mem.reshape(batch * n_mem, d)

    for l in range(depth):
        wl = w_in[l]
        kr_w = wl[:, o_kr:o_sb]
        w1 = jnp.concatenate([
            wl[:, o_cq:o_kr],
            jnp.pad(kr_w, ((0, 0), (MLA_NOPE, LANES - MLA_NOPE - MLA_ROPE))),
            jnp.pad(_swap_halves(kr_w), ((0, 0), (MLA_NOPE, LANES - MLA_NOPE - MLA_ROPE))),
        ], axis=1).astype(BF16)
        uq = w_uq[l].reshape(MLA_Q_RANK, MLA_HEADS, MLA_NOPE + MLA_ROPE)
        uq_nope, uq_rope = uq[..., :MLA_NOPE], uq[..., MLA_NOPE:]
        padq = ((0, 0), (0, 0), (0, LANES - MLA_NOPE - MLA_ROPE))
        wq = jnp.pad(jnp.concatenate([uq_nope, uq_rope], -1), padq).reshape(MLA_Q_RANK, -1).astype(BF16)
        wqs = jnp.pad(jnp.concatenate([jnp.zeros_like(uq_nope), _swap_halves(uq_rope)], -1), padq)
        wqs = wqs.reshape(MLA_Q_RANK, -1).astype(BF16)
        ukv = w_ukv[l].reshape(MLA_KV_RANK, MLA_HEADS, MLA_NOPE + MLA_V)
        wk = _pad_heads(ukv[..., :MLA_NOPE].reshape(MLA_KV_RANK, -1), MLA_HEADS, MLA_NOPE).astype(BF16)
        wv = _pad_heads(ukv[..., MLA_NOPE:].reshape(MLA_KV_RANK, -1), MLA_HEADS, MLA_V).astype(BF16)
        q_a, k_a, v_a = mla_prep(xf, attn_norm[l], w1, q_norm[l], kv_norm[l], wq, wqs, wk, wv,
                                 cos_t, sin_t, tm=tm, seq=seq)

        sb = wl[:, o_sb:o_diff]
        dq = wl[:, o_diff:o_diff + diff_qk].reshape(d, DIFF_HEADS, 2, DIFF_DIM)
        dq = jnp.stack([jnp.pad(dq[:, :, 0], ((0, 0), (0, 0), (0, DIFF_DIM))),
                        jnp.pad(dq[:, :, 1], ((0, 0), (0, 0), (DIFF_DIM, 0)))], axis=2)
        w2 = jnp.concatenate([
            _pad_heads(sb[:, :sb_w], SB_HEADS, SB_DIM),
            _pad_heads(sb[:, sb_w:2 * sb_w], SB_HEADS, SB_DIM),
            _pad_heads(sb[:, 2 * sb_w:], SB_HEADS, SB_DIM),
            dq.reshape(d, 2 * DIFF_HEADS * LANES),
            wl[:, o_diff + diff_qk:o_diff + 3 * diff_qk],
        ], axis=1).astype(BF16)
        p2 = norm_matmul(xf, attn_norm[l], w2, tm=tm_big, tn=w2.shape[1] // 2)
        cb_sb_q, cb_sb_k, cb_sb_v = 0, SB_HEADS, 2 * SB_HEADS
        cb_d_q = 3 * SB_HEADS
        cb_d_k = cb_d_q + 2 * DIFF_HEADS
        cb_d_v = cb_d_k + DIFF_HEADS

        o_a = flash_attention(q_a, k_a, v_a, batch=batch, seq=seq, heads=MLA_HEADS, q_col0=0, k_col0=0,
                              v_col0=0, heads_per_kv=1, tq=tq_attn, tk=tk_attn, ts=ts_attn)
        o_b = sb_attention(p2, p2, p2, batch=batch, seq=seq, heads=SB_HEADS, q_col0=cb_sb_q,
                           k_col0=cb_sb_k, v_col0=cb_sb_v, tq=tq_attn, tk=tk_attn, tw=min(256, tk_attn), ts=ts_sb,
                           q_scale=SB_DIM ** -0.5 * LOG2E)
        slopes = tuple(2.0 ** (-8.0 * (i + 1) / DIFF_HEADS) for i in range(DIFF_HEADS))
        o_c = flash_attention(p2, p2, p2, batch=batch, seq=seq, heads=2 * DIFF_HEADS, q_col0=cb_d_q,
                              k_col0=cb_d_k, v_col0=cb_d_v, heads_per_kv=2, tq=tq_attn, tk=tk_attn, ts=ts_attn,
                              q_scale=DIFF_DIM ** -0.5 * LOG2E, slopes=slopes, heads_per_slope=2, out_dtype=F32)

        lam_init = 0.8 - 0.6 * math.exp(-0.3 * l)
        wba = _pad_head_rows(w_branch[l, 0], MLA_HEADS, MLA_V).astype(BF16)
        wbb = _pad_head_rows(w_branch[l, 1], SB_HEADS, SB_DIM).astype(BF16)
        xf = merge(xf, attn_norm[l], wl[:, :gate_cols].astype(BF16), o_a, o_b, o_c, diff_lambda[l],
                   diff_subln[l], wba, wbb, w_branch[l, 2].astype(BF16), w_out[l].astype(BF16),
                   lam_init=lam_init, tm=min(256, seq))

        kv = norm_matmul(mem_f, mem_norm, w_xkv[l].astype(BF16), tm=min(256, n_mem), tn=512)
        xf = cross_attention(xf, cross_norm[l], w_xq[l].astype(BF16), kv, w_xo[l].astype(BF16),
                             batch=batch, seq=seq, tm=tm)

        pad_r = LANES - N_EXPERTS - N_GROUPS
        wr = jnp.pad(jnp.concatenate([w_router_exp[l], w_router_grp[l]], axis=1), ((0, 0), (0, pad_r)))
        br = jnp.pad(jnp.concatenate([b_router_exp[l], b_router_grp[l]]), (0, pad_r)).reshape(1, LANES)
        comb = router(xf, ffn_norm[l], wr, br, tm=tm)
        xf = moe(xf, ffn_norm[l], comb, w_exp_gate[l].astype(BF16), w_exp_up[l].astype(BF16),
                 w_exp_down[l].astype(BF16), final_norm, tm=tm_big, ec=EXPERTS_PER_GROUP,
                 final_norm=(l == depth - 1))

    return xf.reshape(batch, seq, d)
```

```python
import functools
import math

import jax
import jax.numpy as jnp
from jax import lax
from jax.experimental import pallas as pl
from jax.experimental.pallas import tpu as pltpu

F32 = jnp.float32
BF16 = jnp.bfloat16
EPS = 1e-6
LANES = 128
LOG2E = math.log2(math.e)
VMEM_LIMIT_BYTES = 56 * 1024 * 1024

ROPE_THETA = 10000.0
MLA_HEADS, MLA_NOPE, MLA_ROPE, MLA_V = 8, 64, 32, 64
MLA_Q_RANK, MLA_KV_RANK = 256, 256
SB_HEADS, SB_DIM = 8, 64
DIFF_HEADS, DIFF_DIM = 4, 64
X_HEADS, X_DIM = 4, 128
N_GROUPS, EXPERTS_PER_GROUP = 4, 8
N_EXPERTS = N_GROUPS * EXPERTS_PER_GROUP
GROUP_LANE0 = N_EXPERTS


def _params(*semantics):
    return pltpu.CompilerParams(dimension_semantics=semantics, vmem_limit_bytes=VMEM_LIMIT_BYTES)


def _rms(x, g):
    return x * lax.rsqrt(jnp.mean(x * x, axis=-1, keepdims=True) + EPS) * g


def _dot(a, b):
    return jnp.dot(a, b, preferred_element_type=F32)


def _dot_nt(a, b):
    return lax.dot_general(a, b, (((1,), (1,)), ((), ())), preferred_element_type=F32)


def _norm_matmul_kernel(x_ref, g_ref, w_ref, o_ref, h_scr):
    @pl.when(pl.program_id(1) == 0)
    def _():
        h_scr[...] = _rms(x_ref[...], g_ref[...]).astype(BF16)

    o_ref[...] = _dot(h_scr[...], w_ref[...]).astype(o_ref.dtype)


def norm_matmul(x, g, w, *, tm, tn, out_dtype=BF16):
    n, k = x.shape
    m = w.shape[1]
    return pl.pallas_call(
        _norm_matmul_kernel,
        out_shape=jax.ShapeDtypeStruct((n, m), out_dtype),
        grid=(n // tm, m // tn),
        in_specs=[
            pl.BlockSpec((tm, k), lambda i, j: (i, 0)),
            pl.BlockSpec((1, k), lambda i, j: (0, 0)),
            pl.BlockSpec((k, tn), lambda i, j: (0, j)),
        ],
        out_specs=pl.BlockSpec((tm, tn), lambda i, j: (i, j)),
        scratch_shapes=[pltpu.VMEM((tm, k), BF16)],
        compiler_params=_params("parallel", "arbitrary"),
        name="norm_matmul",
    )(x, g.reshape(1, k), w)


def _mla_prep_kernel(x_ref, g_ref, w1_ref, qn_ref, kvn_ref, wq_ref, wqs_ref, wk_ref, wv_ref,
                     cos_ref, sin_ref, q_ref, k_ref, v_ref, *, scale):
    h = _rms(x_ref[...], g_ref[...]).astype(BF16)
    c = _dot(h, w1_ref[...])
    cq = _rms(c[:, :MLA_Q_RANK], qn_ref[...]).astype(BF16)
    ckv = _rms(c[:, MLA_Q_RANK:MLA_Q_RANK + MLA_KV_RANK], kvn_ref[...]).astype(BF16)
    base = MLA_Q_RANK + MLA_KV_RANK
    cos = cos_ref[...]
    sin = sin_ref[...]
    kr = c[:, base:base + LANES] * cos + c[:, base + LANES:base + 2 * LANES] * sin
    q = _dot(cq, wq_ref[...])
    qs = _dot(cq, wqs_ref[...])
    k = _dot(ckv, wk_ref[...])
    v_ref[...] = _dot(ckv, wv_ref[...]).astype(BF16)
    for hh in range(MLA_HEADS):
        sl = slice(hh * LANES, (hh + 1) * LANES)
        q_ref[:, sl] = ((q[:, sl] * cos + qs[:, sl] * sin) * scale).astype(BF16)
        k_ref[:, sl] = (k[:, sl] + kr).astype(BF16)


def mla_prep(x, g, w1, qn, kvn, wq, wqs, wk, wv, cos_t, sin_t, *, tm, seq):
    n, d = x.shape
    width = MLA_HEADS * LANES
    steps_per_seq = seq // tm
    full = lambda a: pl.BlockSpec(a.shape, lambda i: (0,) * a.ndim)
    out = jax.ShapeDtypeStruct((n, width), BF16)
    row = pl.BlockSpec((tm, width), lambda i: (i, 0))
    args = (g.reshape(1, d), w1, qn.reshape(1, -1), kvn.reshape(1, -1), wq, wqs, wk, wv)
    return pl.pallas_call(
        functools.partial(_mla_prep_kernel, scale=(MLA_NOPE + MLA_ROPE) ** -0.5 * LOG2E),
        out_shape=(out, out, out),
        grid=(n // tm,),
        in_specs=[pl.BlockSpec((tm, d), lambda i: (i, 0))] + [full(a) for a in args] + [
            pl.BlockSpec((tm, LANES), lambda i: (i % steps_per_seq, 0)),
            pl.BlockSpec((tm, LANES), lambda i: (i % steps_per_seq, 0)),
        ],
        out_specs=(row, row, row),
        compiler_params=_params("parallel"),
        name="mla_prep",
    )(x, *args, cos_t, sin_t)


def _flash_kernel(q_ref, k_ref, v_ref, o_ref, s_scr, p_scr, alpha_scr, m_scr, l_scr, acc_scr, *, tq, tk, ts,
                  q_scale, slopes, heads_per_slope):
    qi = pl.program_id(2)
    nchunk = tk // LANES
    n_full = 2 * qi
    m_scr[...] = jnp.full(m_scr.shape, -jnp.inf, F32)
    l_scr[...] = jnp.zeros(l_scr.shape, F32)
    acc_scr[...] = jnp.zeros(acc_scr.shape, F32)
    q = q_ref[...]
    if q_scale != 1.0:
        q = (q.astype(F32) * q_scale).astype(BF16)

    if slopes is not None:
        slope_idx = pl.program_id(1) // heads_per_slope
        slope = jnp.float32(slopes[-1])
        for i in range(len(slopes) - 2, -1, -1):
            slope = jnp.where(slope_idx == i, jnp.float32(slopes[i]), slope)
        col = lax.broadcasted_iota(jnp.int32, (1, tk), 1)
    lane_minus_row = (lax.broadcasted_iota(jnp.int32, (ts, LANES), 1)
                      - lax.broadcasted_iota(jnp.int32, (ts, LANES), 0))

    def scores(kb, slot):
        s = _dot_nt(q, k_ref[pl.ds(pl.multiple_of(kb * tk, tk), tk), :])
        if slopes is not None:
            s = s + (slope * LOG2E) * (col + (kb * tk - qi * tq)).astype(F32)
        s_scr[slot] = s

    def weighted_values(kb, slot):
        pv = _dot(p_scr[slot], v_ref[pl.ds(pl.multiple_of(kb * tk, tk), tk), :])
        acc_scr[...] = alpha_scr[slot] * acc_scr[...] + pv

    def softmax(slot, local):
        for u in range(tq // ts):
            rows = slice(u * ts, (u + 1) * ts)
            last = u * ts + ts - 1
            live = [c for c in range(nchunk) if local is None or local + c * LANES <= last]
            if not live:
                p_scr[slot, rows, :] = jnp.zeros((ts, tk), BF16)
                alpha_scr[slot, rows, :] = jnp.ones((ts, LANES), F32)
                continue
            sc = [s_scr[slot, rows, c * LANES:(c + 1) * LANES] for c in live]
            if local is not None:
                for i, c in enumerate(live):
                    if local + c * LANES + LANES - 1 > u * ts:
                        sc[i] = jnp.where(lane_minus_row <= u * ts - local - c * LANES, sc[i], -jnp.inf)
            m_prev = m_scr[rows, :]
            m_new = jnp.maximum(m_prev, jnp.max(functools.reduce(jnp.maximum, sc), axis=1, keepdims=True))
            alpha = jnp.exp2(m_prev - m_new)
            ps = [jnp.exp2(x - m_new) for x in sc]
            l_scr[rows, :] = alpha * l_scr[rows, :] + jnp.sum(functools.reduce(jnp.add, ps), axis=1, keepdims=True)
            for c, pc in zip(live, ps):
                p_scr[slot, rows, c * LANES:(c + 1) * LANES] = pc.astype(BF16)
            if len(live) < nchunk:
                p_scr[slot, rows, len(live) * LANES:] = jnp.zeros((ts, tk - len(live) * LANES), BF16)
            alpha_scr[slot, rows, :] = alpha
            m_scr[rows, :] = m_new

    last_full = jnp.maximum(n_full - 1, 0)
    scores(n_full, 0)
    scores(n_full + 1, 1)
    softmax(0, 0)
    weighted_values(n_full, 0)
    softmax(1, tk)
    scores(0, 2)

    def trip(j, slot, last=False):
        weighted_values(jnp.where(j == 0, n_full + 1, j - 1), (slot - 1) % 4)
        softmax(slot, None)
        if not last:
            scores(jnp.minimum(j + 1, last_full), (slot + 1) % 4)

    def quad(i, carry):
        trip(4 * i, 2)
        trip(4 * i + 1, 3)
        trip(4 * i + 2, 0)
        trip(4 * i + 3, 1)
        return carry

    lax.fori_loop(0, qi // 2, quad, 0)

    @pl.when(qi % 2 == 1)
    def _():
        trip(n_full - 2, 2)
        trip(n_full - 1, 3, last=True)
        weighted_values(n_full - 1, 3)

    @pl.when(qi % 2 == 0)
    def _():
        weighted_values(jnp.where(qi == 0, 1, n_full - 1), 1)

    o_ref[...] = (acc_scr[...] / l_scr[...]).astype(o_ref.dtype)


def flash_attention(q, k, v, *, batch, seq, heads, q_col0, k_col0, v_col0, heads_per_kv, tq, tk, ts,
                    q_scale=1.0, slopes=None, heads_per_slope=1, out_dtype=BF16):
    nq = seq // tq
    assert tq == 2 * tk
    kernel = functools.partial(_flash_kernel, tq=tq, tk=tk, ts=ts, q_scale=q_scale, slopes=slopes,
                               heads_per_slope=heads_per_slope)
    return pl.pallas_call(
        kernel,
        out_shape=jax.ShapeDtypeStruct((batch * seq, heads * LANES), out_dtype),
        grid=(batch, heads, nq),
        in_specs=[
            pl.BlockSpec((tq, LANES), lambda b, h, i: (b * nq + i, q_col0 + h)),
            pl.BlockSpec((seq, LANES), lambda b, h, i: (b, k_col0 + h // heads_per_kv)),
            pl.BlockSpec((seq, LANES), lambda b, h, i: (b, v_col0 + h // heads_per_kv)),
        ],
        out_specs=pl.BlockSpec((tq, LANES), lambda b, h, i: (b * nq + i, h)),
        scratch_shapes=[pltpu.VMEM((4, tq, tk), F32), pltpu.VMEM((4, tq, tk), BF16),
                        pltpu.VMEM((4, tq, LANES), F32)] + [pltpu.VMEM((tq, LANES), F32)] * 3,
        compiler_params=_params("parallel", "parallel", "arbitrary"),
        name="flash_attention",
    )(q, k, v)


def _sb_kernel(q_ref, k_ref, v_ref, tri_ref, o_ref, z_scr, p_scr, c_scr, acc_scr, *, tq, tk, tw, ts, q_scale):
    qi = pl.program_id(2)
    ratio = tq // tk
    nchunk = tk // LANES
    q = (q_ref[...].astype(F32) * q_scale).astype(BF16)
    tri = tri_ref[...]
    c_scr[...] = jnp.zeros(c_scr.shape, F32)
    acc_scr[...] = jnp.zeros(acc_scr.shape, F32)
    lane_minus_row = (lax.broadcasted_iota(jnp.int32, (ts, LANES), 1)
                      - lax.broadcasted_iota(jnp.int32, (ts, LANES), 0))

    def scores(kc, slot):
        z_scr[slot] = _dot_nt(q, k_ref[pl.ds(pl.multiple_of(kc * tk, tk), tk), :])

    def weighted_values(kc, slot):
        acc_scr[...] += _dot(p_scr[slot], v_ref[pl.ds(pl.multiple_of(kc * tk, tk), tk), :])

    def weights(slot, local):
        for u in range(tq // ts):
            rows = slice(u * ts, (u + 1) * ts)
            def chunk_state(c):
                if local is None or local + c * LANES + LANES - 1 < u * ts:
                    return "full"
                return "dead" if local + c * LANES >= u * ts + ts - 1 else "masked"

            state = [chunk_state(c) for c in range(nchunk)]
            lss, lks, valid = {}, {}, {}
            for c in range(nchunk):
                if state[c] == "dead":
                    continue
                z = z_scr[slot, rows, c * LANES:(c + 1) * LANES]
                lss[c] = jnp.minimum(z, 0.0) - jnp.log2(1.0 + jnp.exp2(-jnp.abs(z)))
                lks[c] = lss[c] - z
                if state[c] == "masked":
                    valid[c] = lane_minus_row < u * ts - local - c * LANES
                    lks[c] = jnp.where(valid[c], lks[c], 0.0)
            later = c_scr[rows, :]
            gc = tw // LANES
            for g in range(nchunk // gc - 1, -1, -1):
                chunks = range(g * gc, (g + 1) * gc)
                live = [c for c in chunks if state[c] != "dead"]
                for c in chunks:
                    if state[c] == "dead":
                        p_scr[slot, rows, c * LANES:(c + 1) * LANES] = jnp.zeros((ts, LANES), BF16)
                if not live:
                    continue
                grp = [lks[c] if c in lks else jnp.zeros((ts, LANES), F32) for c in chunks]
                after = jnp.concatenate([later] * gc, axis=1) + _dot(jnp.concatenate(grp, axis=1).astype(BF16), tri)
                for c in live:
                    i = c - g * gc
                    p = jnp.exp2(lss[c] + after[:, i * LANES:(i + 1) * LANES])
                    if state[c] == "masked":
                        p = jnp.where(valid[c], p, 0.0)
                    p_scr[slot, rows, c * LANES:(c + 1) * LANES] = p.astype(BF16)
                later = later + jnp.sum(functools.reduce(jnp.add, [lks[c] for c in live]), axis=1, keepdims=True)
            if lks:
                c_scr[rows, :] = later

    n = 2 * (qi + 1)
    scores(n - 1, 0)
    weights(0, tk)
    scores(n - 2, 1)
    weighted_values(n - 1, 0)
    weights(1, 0)
    scores(jnp.maximum(n - 3, 0), 2)

    def trip(t, slot, last=False):
        kc = n - 1 - t
        weighted_values(kc + 1, (slot - 1) % 4)
        weights(slot, None)
        if not last:
            scores(jnp.maximum(kc - 1, 0), (slot + 1) % 4)

    def quad(i, carry):
        t = 2 + 4 * i
        trip(t, 2)
        trip(t + 1, 3)
        trip(t + 2, 0)
        trip(t + 3, 1)
        return carry

    lax.fori_loop(0, qi // 2, quad, 0)

    @pl.when(qi % 2 == 1)
    def _():
        t = 2 + 4 * (qi // 2)
        trip(t, 2)
        trip(t + 1, 3, last=True)
        weighted_values(0, 3)

    @pl.when(qi % 2 == 0)
    def _():
        weighted_values(0, 1)

    o_ref[...] = acc_scr[...].astype(o_ref.dtype)


def sb_attention(q, k, v, *, batch, seq, heads, q_col0, k_col0, v_col0, tq, tk, tw, ts, q_scale):
    assert tq == 2 * tk and tk % tw == 0
    nq = seq // tq
    r = lax.broadcasted_iota(jnp.int32, (tw, tw), 0)
    c = lax.broadcasted_iota(jnp.int32, (tw, tw), 1)
    tri = (r > c).astype(BF16)
    return pl.pallas_call(
        functools.partial(_sb_kernel, tq=tq, tk=tk, tw=tw, ts=ts, q_scale=q_scale),
        out_shape=jax.ShapeDtypeStruct((batch * seq, heads * LANES), BF16),
        grid=(batch, heads, nq),
        in_specs=[
            pl.BlockSpec((tq, LANES), lambda b, h, i: (b * nq + i, q_col0 + h)),
            pl.BlockSpec((seq, LANES), lambda b, h, i: (b, k_col0 + h)),
            pl.BlockSpec((seq, LANES), lambda b, h, i: (b, v_col0 + h)),
            pl.BlockSpec((tw, tw), lambda b, h, i: (0, 0)),
        ],
        out_specs=pl.BlockSpec((tq, LANES), lambda b, h, i: (b * nq + i, h)),
        scratch_shapes=[pltpu.VMEM((4, tq, tk), F32), pltpu.VMEM((4, tq, tk), BF16),
                        pltpu.VMEM((tq, LANES), F32), pltpu.VMEM((tq, LANES), F32)],
        compiler_params=_params("parallel", "parallel", "arbitrary"),
        name="sb_attention",
    )(q, k, v, tri)


def _merge_kernel(x_ref, g_ref, wg_ref, oa_ref, ob_ref, oc_ref, lam_ref, subln_ref,
                  wba_ref, wbb_ref, wbc_ref, wo_ref, o_ref, *, lam_init):
    x = x_ref[...]
    d = x.shape[1]
    h = _rms(x, g_ref[...]).astype(BF16)
    lp = lam_ref[...]
    lam = (jnp.exp(jnp.sum(lp[0:1] * lp[1:2], axis=1, keepdims=True))
           - jnp.exp(jnp.sum(lp[2:3] * lp[3:4], axis=1, keepdims=True)) + lam_init)
    oc = oc_ref[...]
    parts = []
    for hh in range(DIFF_HEADS):
        p1 = oc[:, (2 * hh) * LANES:(2 * hh + 1) * LANES]
        p2 = oc[:, (2 * hh + 1) * LANES:(2 * hh + 2) * LANES]
        parts.append((_rms(p1 - lam * p2, subln_ref[...]) * (1.0 - lam_init)).astype(BF16))
    ocn = jnp.concatenate(parts, axis=1)

    def gate(i):
        return jax.nn.sigmoid(_dot(h, wg_ref[:, i * d:(i + 1) * d]))

    merged = gate(0) * _dot(oa_ref[...], wba_ref[...])
    merged += gate(1) * _dot(ob_ref[...], wbb_ref[...])
    merged += gate(2) * _dot(ocn, wbc_ref[...])
    o_ref[...] = x + _dot(merged.astype(BF16), wo_ref[...])


def merge(x, g, wg, oa, ob, oc, lam_p, subln, wba, wbb, wbc, wo, *, lam_init, tm):
    n, d = x.shape
    full = lambda a: pl.BlockSpec(a.shape, lambda i: (0,) * a.ndim)
    row = lambda a: pl.BlockSpec((tm, a.shape[1]), lambda i: (i, 0))
    g2 = g.reshape(1, d)
    subln2 = subln.reshape(1, -1)
    return pl.pallas_call(
        functools.partial(_merge_kernel, lam_init=lam_init),
        out_shape=jax.ShapeDtypeStruct((n, d), F32),
        grid=(n // tm,),
        in_specs=[row(x), full(g2), full(wg), row(oa), row(ob), row(oc), full(lam_p), full(subln2),
                  full(wba), full(wbb), full(wbc), full(wo)],
        out_specs=row(x),
        compiler_params=_params("parallel"),
        name="merge",
    )(x, g2, wg, oa, ob, oc, lam_p, subln2, wba, wbb, wbc, wo)


def _cross_kernel(x_ref, g_ref, wq_ref, k_ref, v_ref, wo_ref, o_ref, *, scale):
    x = x_ref[...]
    h = _rms(x, g_ref[...]).astype(BF16)
    q = (_dot(h, wq_ref[...]) * scale).astype(BF16)
    outs = []
    for hh in range(X_HEADS):
        sl = slice(hh * X_DIM, (hh + 1) * X_DIM)
        s = _dot_nt(q[:, sl], k_ref[:, sl])
        p = jnp.exp(s - jnp.max(s, axis=1, keepdims=True))
        o = _dot(p.astype(BF16), v_ref[:, sl]) / jnp.sum(p, axis=1, keepdims=True)
        outs.append(o.astype(BF16))
    o_ref[...] = x + _dot(jnp.concatenate(outs, axis=1), wo_ref[...])


def cross_attention(x, g, wq, kv, wo, *, batch, seq, tm):
    n, d = x.shape
    n_mem = kv.shape[0] // batch
    width = X_HEADS * X_DIM
    steps = seq // tm
    full = lambda a: pl.BlockSpec(a.shape, lambda b, i: (0,) * a.ndim)
    g2 = g.reshape(1, d)
    return pl.pallas_call(
        functools.partial(_cross_kernel, scale=X_DIM ** -0.5),
        out_shape=jax.ShapeDtypeStruct((n, d), F32),
        grid=(batch, steps),
        in_specs=[
            pl.BlockSpec((tm, d), lambda b, i: (b * steps + i, 0)),
            full(g2), full(wq),
            pl.BlockSpec((n_mem, width), lambda b, i: (b, 0)),
            pl.BlockSpec((n_mem, width), lambda b, i: (b, 1)),
            full(wo),
        ],
        out_specs=pl.BlockSpec((tm, d), lambda b, i: (b * steps + i, 0)),
        compiler_params=_params("parallel", "parallel"),
        name="cross_attention",
    )(x, g2, wq, kv, kv, wo)


def _router_kernel(x_ref, g_ref, w_ref, b_ref, comb_ref):
    h = _rms(x_ref[...], g_ref[...])
    logits = jnp.dot(h, w_ref[...], preferred_element_type=F32, precision=lax.Precision.HIGHEST) + b_ref[...]
    lane = lax.broadcasted_iota(jnp.int32, logits.shape, 1).astype(F32)
    neg = -jnp.inf
    far = float(4 * LANES)

    def first_argmax(vals, vmax):
        return jnp.min(jnp.where(vals == vmax, lane, far), axis=1, keepdims=True)

    gl = jnp.where((lane >= GROUP_LANE0) & (lane < GROUP_LANE0 + N_GROUPS), logits, neg)
    gmax = jnp.max(gl, axis=1, keepdims=True)
    lo = (first_argmax(gl, gmax) - GROUP_LANE0) * EXPERTS_PER_GROUP
    p_g = 1.0 / jnp.sum(jnp.exp(gl - gmax), axis=1, keepdims=True)
    el = jnp.where((lane >= lo) & (lane < lo + EXPERTS_PER_GROUP), logits, neg)
    v1 = jnp.max(el, axis=1, keepdims=True)
    i1 = first_argmax(el, v1)
    el2 = jnp.where(lane == i1, neg, el)
    v2 = jnp.max(el2, axis=1, keepdims=True)
    i2 = first_argmax(el2, v2)
    t = jnp.exp(v2 - v1)
    w1 = 1.0 / (1.0 + t)
    comb_ref[...] = jnp.where(lane == i1, w1 * p_g, 0.0) + jnp.where(lane == i2, (t * w1) * p_g, 0.0)


def router(x, g, w, b, *, tm):
    n, d = x.shape
    full = lambda a: pl.BlockSpec(a.shape, lambda i: (0,) * a.ndim)
    g2 = g.reshape(1, d)
    return pl.pallas_call(
        _router_kernel,
        out_shape=jax.ShapeDtypeStruct((n, LANES), F32),
        grid=(n // tm,),
        in_specs=[pl.BlockSpec((tm, d), lambda i: (i, 0)), full(g2), full(w), full(b)],
        out_specs=pl.BlockSpec((tm, LANES), lambda i: (i, 0)),
        compiler_params=_params("parallel"),
        name="router",
    )(x, g2, w, b)


def _moe_kernel(x_ref, g_ref, comb_ref, wg_ref, wu_ref, wd_ref, fg_ref, o_ref, h_scr, y_scr, *, ec, final_norm):
    j = pl.program_id(1)

    @pl.when(j == 0)
    def _():
        h_scr[...] = _rms(x_ref[...], g_ref[...]).astype(BF16)
        y_scr[...] = jnp.zeros(y_scr.shape, F32)

    h = h_scr[...]
    comb = comb_ref[...]
    lane = lax.broadcasted_iota(jnp.int32, comb.shape, 1)
    for e in range(ec):
        a = _dot(h, wg_ref[e])
        u = _dot(h, wu_ref[e])
        cw = jnp.sum(jnp.where(lane == j * ec + e, comb, 0.0), axis=1, keepdims=True)
        act = (a * jax.nn.sigmoid(a)) * u * cw
        y_scr[...] += _dot(act.astype(BF16), wd_ref[e])

    @pl.when(j == pl.num_programs(1) - 1)
    def _():
        y = x_ref[...] + y_scr[...]
        if final_norm:
            y = _rms(y, fg_ref[...])
        o_ref[...] = y


def moe(x, g, comb, wg, wu, wd, fg, *, tm, ec, final_norm):
    n, d = x.shape
    n_exp, _, f = wg.shape
    g2 = g.reshape(1, d)
    fg2 = fg.reshape(1, d)
    return pl.pallas_call(
        functools.partial(_moe_kernel, ec=ec, final_norm=final_norm),
        out_shape=jax.ShapeDtypeStruct((n, d), F32),
        grid=(n // tm, n_exp // ec),
        in_specs=[
            pl.BlockSpec((tm, d), lambda i, j: (i, 0)),
            pl.BlockSpec((1, d), lambda i, j: (0, 0)),
            pl.BlockSpec((tm, LANES), lambda i, j: (i, 0)),
            pl.BlockSpec((ec, d, f), lambda i, j: (j, 0, 0)),
            pl.BlockSpec((ec, d, f), lambda i, j: (j, 0, 0)),
            pl.BlockSpec((ec, f, d), lambda i, j: (j, 0, 0)),
            pl.BlockSpec((1, d), lambda i, j: (0, 0)),
        ],
        out_specs=pl.BlockSpec((tm, d), lambda i, j: (i, 0)),
        scratch_shapes=[pltpu.VMEM((tm, d), BF16), pltpu.VMEM((tm, d), F32)],
        compiler_params=_params("parallel", "arbitrary"),
        name="moe",
    )(x, g2, comb, wg, wu, wd, fg2)


def _pad_heads(w, heads, dim, lane0=0):
    k = w.shape[0]
    w = w.reshape(k, heads, dim)
    w = jnp.pad(w, ((0, 0), (0, 0), (lane0, LANES - dim - lane0)))
    return w.reshape(k, heads * LANES)


def _pad_head_rows(w, heads, dim):
    m = w.shape[1]
    w = w.reshape(heads, dim, m)
    w = jnp.pad(w, ((0, 0), (0, LANES - dim), (0, 0)))
    return w.reshape(heads * LANES, m)


def _swap_halves(w):
    half = w.shape[-1] // 2
    return jnp.concatenate([w[..., half:], w[..., :half]], axis=-1)


def _rope_tables(seq):
    half = MLA_ROPE // 2
    inv_freq = ROPE_THETA ** (-jnp.arange(half, dtype=F32) / half)
    ang = jnp.arange(seq, dtype=F32)[:, None] * inv_freq[None, :]
    cos, sin = jnp.cos(ang), jnp.sin(ang)
    ones = jnp.ones((seq, MLA_NOPE), F32)
    zn = jnp.zeros((seq, MLA_NOPE), F32)
    zp = jnp.zeros((seq, LANES - MLA_NOPE - MLA_ROPE), F32)
    return (jnp.concatenate([ones, cos, cos, zp], axis=1),
            jnp.concatenate([zn, -sin, sin, zp], axis=1))


def kernel(x, mem, attn_norm, w_in, q_norm, w_uq, kv_norm, w_ukv, diff_lambda, diff_subln, w_branch, w_out,
           cross_norm, mem_norm, w_xq, w_xkv, w_xo, ffn_norm, w_router_grp, b_router_grp, w_router_exp,
           b_router_exp, w_exp_gate, w_exp_up, w_exp_down, final_norm):
    batch, seq, d = x.shape
    depth = w_in.shape[0]
    n = batch * seq
    n_mem = mem.shape[1]
    tm = min(512, seq)
    tm_big = min(1024, seq)
    tq_attn = min(1024, seq)
    tk_attn = tq_attn // 2
    ts_attn = min(64, seq)
    ts_sb = min(256, seq)

    gate_cols = 3 * d
    o_cq = gate_cols
    o_ckv = o_cq + MLA_Q_RANK
    o_kr = o_ckv + MLA_KV_RANK
    o_sb = o_kr + MLA_ROPE
    sb_w = SB_HEADS * SB_DIM
    o_diff = o_sb + 3 * sb_w
    diff_qk = DIFF_HEADS * 2 * DIFF_DIM

    cos_t, sin_t = _rope_tables(seq)
    xf = x.reshape(n, d)
    mem_f = mem.reshape(batch * n_mem, d)

    for l in range(depth):
        wl = w_in[l]
        kr_w = wl[:, o_kr:o_sb]
        w1 = jnp.concatenate([
            wl[:, o_cq:o_kr],
            jnp.pad(kr_w, ((0, 0), (MLA_NOPE, LANES - MLA_NOPE - MLA_ROPE))),
            jnp.pad(_swap_halves(kr_w), ((0, 0), (MLA_NOPE, LANES - MLA_NOPE - MLA_ROPE))),
        ], axis=1).astype(BF16)
        uq = w_uq[l].reshape(MLA_Q_RANK, MLA_HEADS, MLA_NOPE + MLA_ROPE)
        uq_nope, uq_rope = uq[..., :MLA_NOPE], uq[..., MLA_NOPE:]
        padq = ((0, 0), (0, 0), (0, LANES - MLA_NOPE - MLA_ROPE))
        wq = jnp.pad(jnp.concatenate([uq_nope, uq_rope], -1), padq).reshape(MLA_Q_RANK, -1).astype(BF16)
        wqs = jnp.pad(jnp.concatenate([jnp.zeros_like(uq_nope), _swap_halves(uq_rope)], -1), padq)
        wqs = wqs.reshape(MLA_Q_RANK, -1).astype(BF16)
        ukv = w_ukv[l].reshape(MLA_KV_RANK, MLA_HEADS, MLA_NOPE + MLA_V)
        wk = _pad_heads(ukv[..., :MLA_NOPE].reshape(MLA_KV_RANK, -1), MLA_HEADS, MLA_NOPE).astype(BF16)
        wv = _pad_heads(ukv[..., MLA_NOPE:].reshape(MLA_KV_RANK, -1), MLA_HEADS, MLA_V).astype(BF16)
        q_a, k_a, v_a = mla_prep(xf, attn_norm[l], w1, q_norm[l], kv_norm[l], wq, wqs, wk, wv,
                                 cos_t, sin_t, tm=tm, seq=seq)

        sb = wl[:, o_sb:o_diff]
        dq = wl[:, o_diff:o_diff + diff_qk].reshape(d, DIFF_HEADS, 2, DIFF_DIM)
        dq = jnp.stack([jnp.pad(dq[:, :, 0], ((0, 0), (0, 0), (0, DIFF_DIM))),
                        jnp.pad(dq[:, :, 1], ((0, 0), (0, 0), (DIFF_DIM, 0)))], axis=2)
        w2 = jnp.concatenate([
            _pad_heads(sb[:, :sb_w], SB_HEADS, SB_DIM),
            _pad_heads(sb[:, sb_w:2 * sb_w], SB_HEADS, SB_DIM),
            _pad_heads(sb[:, 2 * sb_w:], SB_HEADS, SB_DIM),
            dq.reshape(d, 2 * DIFF_HEADS * LANES),
            wl[:, o_diff + diff_qk:o_diff + 3 * diff_qk],
        ], axis=1).astype(BF16)
        p2 = norm_matmul(xf, attn_norm[l], w2, tm=tm_big, tn=w2.shape[1] // 2)
        cb_sb_q, cb_sb_k, cb_sb_v = 0, SB_HEADS, 2 * SB_HEADS
        cb_d_q = 3 * SB_HEADS
        cb_d_k = cb_d_q + 2 * DIFF_HEADS
        cb_d_v = cb_d_k + DIFF_HEADS

        o_a = flash_attention(q_a, k_a, v_a, batch=batch, seq=seq, heads=MLA_HEADS, q_col0=0, k_col0=0,
                              v_col0=0, heads_per_kv=1, tq=tq_attn, tk=tk_attn, ts=ts_attn)
        o_b = sb_attention(p2, p2, p2, batch=batch, seq=seq, heads=SB_HEADS, q_col0=cb_sb_q,
                           k_col0=cb_sb_k, v_col0=cb_sb_v, tq=tq_attn, tk=tk_attn, tw=min(256, tk_attn), ts=ts_sb,
                           q_scale=SB_DIM ** -0.5 * LOG2E)
        slopes = tuple(2.0 ** (-8.0 * (i + 1) / DIFF_HEADS) for i in range(DIFF_HEADS))
        o_c = flash_attention(p2, p2, p2, batch=batch, seq=seq, heads=2 * DIFF_HEADS, q_col0=cb_d_q,
                              k_col0=cb_d_k, v_col0=cb_d_v, heads_per_kv=2, tq=tq_attn, tk=tk_attn, ts=ts_attn,
                              q_scale=DIFF_DIM ** -0.5 * LOG2E, slopes=slopes, heads_per_slope=2, out_dtype=F32)

        lam_init = 0.8 - 0.6 * math.exp(-0.3 * l)
        wba = _pad_head_rows(w_branch[l, 0], MLA_HEADS, MLA_V).astype(BF16)
        wbb = _pad_head_rows(w_branch[l, 1], SB_HEADS, SB_DIM).astype(BF16)
        xf = merge(xf, attn_norm[l], wl[:, :gate_cols].astype(BF16), o_a, o_b, o_c, diff_lambda[l],
                   diff_subln[l], wba, wbb, w_branch[l, 2].astype(BF16), w_out[l].astype(BF16),
                   lam_init=lam_init, tm=min(256, seq))

        kv = norm_matmul(mem_f, mem_norm, w_xkv[l].astype(BF16), tm=min(256, n_mem), tn=512)
        xf = cross_attention(xf, cross_norm[l], w_xq[l].astype(BF16), kv, w_xo[l].astype(BF16),
                             batch=batch, seq=seq, tm=tm)

        pad_r = LANES - N_EXPERTS - N_GROUPS
        wr = jnp.pad(jnp.concatenate([w_router_exp[l], w_router_grp[l]], axis=1), ((0, 0), (0, pad_r)))
        br = jnp.pad(jnp.concatenate([b_router_exp[l], b_router_grp[l]]), (0, pad_r)).reshape(1, LANES)
        comb = router(xf, ffn_norm[l], wr, br, tm=tm)
        xf = moe(xf, ffn_norm[l], comb, w_exp_gate[l].astype(BF16), w_exp_up[l].astype(BF16),
                 w_exp_down[l].astype(BF16), final_norm, tm=tm_big, ec=EXPERTS_PER_GROUP,
                 final_norm=(l == depth - 1))

    return xf.reshape(batch, seq, d)
```

```python
import functools
import math

import jax
import jax.numpy as jnp
from jax import lax
from jax.experimental import pallas as pl
from jax.experimental.pallas import tpu as pltpu

F32 = jnp.float32
BF16 = jnp.bfloat16
EPS = 1e-6
LANES = 128
LOG2E = math.log2(math.e)
VMEM_LIMIT_BYTES = 56 * 1024 * 1024

ROPE_THETA = 10000.0
MLA_HEADS, MLA_NOPE, MLA_ROPE, MLA_V = 8, 64, 32, 64
MLA_Q_RANK, MLA_KV_RANK = 256, 256
SB_HEADS, SB_DIM = 8, 64
DIFF_HEADS, DIFF_DIM = 4, 64
X_HEADS, X_DIM = 4, 128
N_GROUPS, EXPERTS_PER_GROUP = 4, 8
N_EXPERTS = N_GROUPS * EXPERTS_PER_GROUP
GROUP_LANE0 = N_EXPERTS


def _params(*semantics):
    return pltpu.CompilerParams(dimension_semantics=semantics, vmem_limit_bytes=VMEM_LIMIT_BYTES)


def _rms(x, g):
    return x * lax.rsqrt(jnp.mean(x * x, axis=-1, keepdims=True) + EPS) * g


def _dot(a, b):
    return jnp.dot(a, b, preferred_element_type=F32)


def _dot_nt(a, b):
    return lax.dot_general(a, b, (((1,), (1,)), ((), ())), preferred_element_type=F32)


def _norm_matmul_kernel(x_ref, g_ref, w_ref, o_ref, h_scr):
    @pl.when(pl.program_id(1) == 0)
    def _():
        h_scr[...] = _rms(x_ref[...], g_ref[...]).astype(BF16)

    o_ref[...] = _dot(h_scr[...], w_ref[...]).astype(o_ref.dtype)


def norm_matmul(x, g, w, *, tm, tn, out_dtype=BF16):
    n, k = x.shape
    m = w.shape[1]
    return pl.pallas_call(
        _norm_matmul_kernel,
        out_shape=jax.ShapeDtypeStruct((n, m), out_dtype),
        grid=(n // tm, m // tn),
        in_specs=[
            pl.BlockSpec((tm, k), lambda i, j: (i, 0)),
            pl.BlockSpec((1, k), lambda i, j: (0, 0)),
            pl.BlockSpec((k, tn), lambda i, j: (0, j)),
        ],
        out_specs=pl.BlockSpec((tm, tn), lambda i, j: (i, j)),
        scratch_shapes=[pltpu.VMEM((tm, k), BF16)],
        compiler_params=_params("parallel", "arbitrary"),
        name="norm_matmul",
    )(x, g.reshape(1, k), w)


def _mla_prep_kernel(x_ref, g_ref, w1_ref, qn_ref, kvn_ref, wq_ref, wqs_ref, wk_ref, wv_ref,
                     cos_ref, sin_ref, q_ref, k_ref, v_ref, *, scale):
    h = _rms(x_ref[...], g_ref[...]).astype(BF16)
    c = _dot(h, w1_ref[...])
    cq = _rms(c[:, :MLA_Q_RANK], qn_ref[...]).astype(BF16)
    ckv = _rms(c[:, MLA_Q_RANK:MLA_Q_RANK + MLA_KV_RANK], kvn_ref[...]).astype(BF16)
    base = MLA_Q_RANK + MLA_KV_RANK
    cos = cos_ref[...]
    sin = sin_ref[...]
    kr = c[:, base:base + LANES] * cos + c[:, base + LANES:base + 2 * LANES] * sin
    q = _dot(cq, wq_ref[...])
    qs = _dot(cq, wqs_ref[...])
    k = _dot(ckv, wk_ref[...])
    v_ref[...] = _dot(ckv, wv_ref[...]).astype(BF16)
    for hh in range(MLA_HEADS):
        sl = slice(hh * LANES, (hh + 1) * LANES)
        q_ref[:, sl] = ((q[:, sl] * cos + qs[:, sl] * sin) * scale).astype(BF16)
        k_ref[:, sl] = (k[:, sl] + kr).astype(BF16)


def mla_prep(x, g, w1, qn, kvn, wq, wqs, wk, wv, cos_t, sin_t, *, tm, seq):
    n, d = x.shape
    width = MLA_HEADS * LANES
    steps_per_seq = seq // tm
    full = lambda a: pl.BlockSpec(a.shape, lambda i: (0,) * a.ndim)
    out = jax.ShapeDtypeStruct((n, width), BF16)
    row = pl.BlockSpec((tm, width), lambda i: (i, 0))
    args = (g.reshape(1, d), w1, qn.reshape(1, -1), kvn.reshape(1, -1), wq, wqs, wk, wv)
    return pl.pallas_call(
        functools.partial(_mla_prep_kernel, scale=(MLA_NOPE + MLA_ROPE) ** -0.5 * LOG2E),
        out_shape=(out, out, out),
        grid=(n // tm,),
        in_specs=[pl.BlockSpec((tm, d), lambda i: (i, 0))] + [full(a) for a in args] + [
            pl.BlockSpec((tm, LANES), lambda i: (i % steps_per_seq, 0)),
            pl.BlockSpec((tm, LANES), lambda i: (i % steps_per_seq, 0)),
        ],
        out_specs=(row, row, row),
        compiler_params=_params("parallel"),
        name="mla_prep",
    )(x, *args, cos_t, sin_t)


def _flash_kernel(q_ref, k_ref, v_ref, o_ref, s_scr, p_scr, alpha_scr, m_scr, l_scr, acc_scr, *, tq, tk, ts,
                  q_scale, slopes, heads_per_slope):
    qi = pl.program_id(2)
    nchunk = tk // LANES
    n_full = 2 * qi
    m_scr[...] = jnp.full(m_scr.shape, -jnp.inf, F32)
    l_scr[...] = jnp.zeros(l_scr.shape, F32)
    acc_scr[...] = jnp.zeros(acc_scr.shape, F32)
    q = q_ref[...]
    if q_scale != 1.0:
        q = (q.astype(F32) * q_scale).astype(BF16)

    if slopes is not None:
        slope_idx = pl.program_id(1) // heads_per_slope
        slope = jnp.float32(slopes[-1])
        for i in range(len(slopes) - 2, -1, -1):
            slope = jnp.where(slope_idx == i, jnp.float32(slopes[i]), slope)
        col = lax.broadcasted_iota(jnp.int32, (1, tk), 1)
    lane_minus_row = (lax.broadcasted_iota(jnp.int32, (ts, LANES), 1)
                      - lax.broadcasted_iota(jnp.int32, (ts, LANES), 0))

    def scores(kb, slot):
        s = _dot_nt(q, k_ref[pl.ds(pl.multiple_of(kb * tk, tk), tk), :])
        if slopes is not None:
            s = s + (slope * LOG2E) * (col + (kb * tk - qi * tq)).astype(F32)
        s_scr[slot] = s

    def weighted_values(kb, slot):
        pv = _dot(p_scr[slot], v_ref[pl.ds(pl.multiple_of(kb * tk, tk), tk), :])
        acc_scr[...] = alpha_scr[slot] * acc_scr[...] + pv

    def softmax(slot, local):
        for u in range(tq // ts):
            rows = slice(u * ts, (u + 1) * ts)
            last = u * ts + ts - 1
            live = [c for c in range(nchunk) if local is None or local + c * LANES <= last]
            if not live:
                p_scr[slot, rows, :] = jnp.zeros((ts, tk), BF16)
                alpha_scr[slot, rows, :] = jnp.ones((ts, LANES), F32)
                continue
            sc = [s_scr[slot, rows, c * LANES:(c + 1) * LANES] for c in live]
            if local is not None:
                for i, c in enumerate(live):
                    if local + c * LANES + LANES - 1 > u * ts:
                        sc[i] = jnp.where(lane_minus_row <= u * ts - local - c * LANES, sc[i], -jnp.inf)
            m_prev = m_scr[rows, :]
            m_new = jnp.maximum(m_prev, jnp.max(functools.reduce(jnp.maximum, sc), axis=1, keepdims=True))
            alpha = jnp.exp2(m_prev - m_new)
            ps = [jnp.exp2(x - m_new) for x in sc]
            l_scr[rows, :] = alpha * l_scr[rows, :] + jnp.sum(functools.reduce(jnp.add, ps), axis=1, keepdims=True)
            for c, pc in zip(live, ps):
                p_scr[slot, rows, c * LANES:(c + 1) * LANES] = pc.astype(BF16)
            if len(live) < nchunk:
                p_scr[slot, rows, len(live) * LANES:] = jnp.zeros((ts, tk - len(live) * LANES), BF16)
            alpha_scr[slot, rows, :] = alpha
            m_scr[rows, :] = m_new

    last_full = jnp.maximum(n_full - 1, 0)
    scores(n_full, 0)
    scores(n_full + 1, 1)
    softmax(0, 0)
    weighted_values(n_full, 0)
    softmax(1, tk)
    scores(0, 2)

    def trip(j, slot, last=False):
        weighted_values(jnp.where(j == 0, n_full + 1, j - 1), (slot - 1) % 4)
        softmax(slot, None)
        if not last:
            scores(jnp.minimum(j + 1, last_full), (slot + 1) % 4)

    def quad(i, carry):
        trip(4 * i, 2)
        trip(4 * i + 1, 3)
        trip(4 * i + 2, 0)
        trip(4 * i + 3, 1)
        return carry

    lax.fori_loop(0, qi // 2, quad, 0)

    @pl.when(qi % 2 == 1)
    def _():
        trip(n_full - 2, 2)
        trip(n_full - 1, 3, last=True)
        weighted_values(n_full - 1, 3)

    @pl.when(qi % 2 == 0)
    def _():
        weighted_values(jnp.where(qi == 0, 1, n_full - 1), 1)

    o_ref[...] = (acc_scr[...] / l_scr[...]).astype(o_ref.dtype)


def flash_attention(q, k, v, *, batch, seq, heads, q_col0, k_col0, v_col0, heads_per_kv, tq, tk, ts,
                    q_scale=1.0, slopes=None, heads_per_slope=1, out_dtype=BF16):
    nq = seq // tq
    assert tq == 2 * tk
    kernel = functools.partial(_flash_kernel, tq=tq, tk=tk, ts=ts, q_scale=q_scale, slopes=slopes,
                               heads_per_slope=heads_per_slope)
    return pl.pallas_call(
        kernel,
        out_shape=jax.ShapeDtypeStruct((batch * seq, heads * LANES), out_dtype),
        grid=(batch, heads, nq),
        in_specs=[
            pl.BlockSpec((tq, LANES), lambda b, h, i: (b * nq + i, q_col0 + h)),
            pl.BlockSpec((seq, LANES), lambda b, h, i: (b, k_col0 + h // heads_per_kv)),
            pl.BlockSpec((seq, LANES), lambda b, h, i: (b, v_col0 + h // heads_per_kv)),
        ],
        out_specs=pl.BlockSpec((tq, LANES), lambda b, h, i: (b * nq + i, h)),
        scratch_shapes=[pltpu.VMEM((4, tq, tk), F32), pltpu.VMEM((4, tq, tk), BF16),
                        pltpu.VMEM((4, tq, LANES), F32)] + [pltpu.VMEM((tq, LANES), F32)] * 3,
        compiler_params=_params("parallel", "parallel", "arbitrary"),
        name="flash_attention",
    )(q, k, v)


def _sb_kernel(q_ref, k_ref, v_ref, tri_ref, o_ref, z_scr, p_scr, c_scr, acc_scr, *, tq, tk, tw, ts, q_scale):
    qi = pl.program_id(2)
    nchunk = tk // LANES
    q = (q_ref[...].astype(F32) * q_scale).astype(BF16)
    tri = tri_ref[...]
    c_scr[...] = jnp.zeros(c_scr.shape, F32)
    acc_scr[...] = jnp.zeros(acc_scr.shape, F32)
    lane_minus_row = (lax.broadcasted_iota(jnp.int32, (ts, LANES), 1)
                      - lax.broadcasted_iota(jnp.int32, (ts, LANES), 0))

    def scores(kc, slot):
        z_scr[slot] = _dot_nt(q, k_ref[pl.ds(pl.multiple_of(kc * tk, tk), tk), :])

    def weighted_values(kc, slot):
        acc_scr[...] += _dot(p_scr[slot], v_ref[pl.ds(pl.multiple_of(kc * tk, tk), tk), :])

    def weights(slot, local):
        for u in range(tq // ts):
            rows = slice(u * ts, (u + 1) * ts)
            def chunk_state(c):
                if local is None or local + c * LANES + LANES - 1 < u * ts:
                    return "full"
                return "dead" if local + c * LANES >= u * ts + ts - 1 else "masked"

            state = [chunk_state(c) for c in range(nchunk)]
            lss, lks, valid = {}, {}, {}
            for c in range(nchunk):
                if state[c] == "dead":
                    continue
                z = z_scr[slot, rows, c * LANES:(c + 1) * LANES]
                lss[c] = jnp.minimum(z, 0.0) - jnp.log2(1.0 + jnp.exp2(-jnp.abs(z)))
                lks[c] = lss[c] - z
                if state[c] == "masked":
                    valid[c] = lane_minus_row < u * ts - local - c * LANES
                    lks[c] = jnp.where(valid[c], lks[c], 0.0)
            later = c_scr[rows, :]
            gc = tw // LANES
            for g in range(nchunk // gc - 1, -1, -1):
                chunks = range(g * gc, (g + 1) * gc)
                live = [c for c in chunks if state[c] != "dead"]
                for c in chunks:
                    if state[c] == "dead":
                        p_scr[slot, rows, c * LANES:(c + 1) * LANES] = jnp.zeros((ts, LANES), BF16)
                if not live:
                    continue
                grp = [lks[c] if c in lks else jnp.zeros((ts, LANES), F32) for c in chunks]
                after = jnp.concatenate([later] * gc, axis=1) + _dot(jnp.concatenate(grp, axis=1).astype(BF16), tri)
                for c in live:
                    i = c - g * gc
                    p = jnp.exp2(lss[c] + after[:, i * LANES:(i + 1) * LANES])
                    if state[c] == "masked":
                        p = jnp.where(valid[c], p, 0.0)
                    p_scr[slot, rows, c * LANES:(c + 1) * LANES] = p.astype(BF16)
                later = later + jnp.sum(functools.reduce(jnp.add, [lks[c] for c in live]), axis=1, keepdims=True)
            if lks:
                c_scr[rows, :] = later

    n = 2 * (qi + 1)
    scores(n - 1, 0)
    weights(0, tk)
    scores(n - 2, 1)
    weighted_values(n - 1, 0)
    weights(1, 0)
    scores(jnp.maximum(n - 3, 0), 2)

    def trip(t, slot, last=False):
        kc = n - 1 - t
        weighted_values(kc + 1, (slot - 1) % 4)
        weights(slot, None)
        if not last:
            scores(jnp.maximum(kc - 1, 0), (slot + 1) % 4)

    def quad(i, carry):
        t = 2 + 4 * i
        trip(t, 2)
        trip(t + 1, 3)
        trip(t + 2, 0)
        trip(t + 3, 1)
        return carry

    lax.fori_loop(0, qi // 2, quad, 0)

    @pl.when(qi % 2 == 1)
    def _():
        t = 2 + 4 * (qi // 2)
        trip(t, 2)
        trip(t + 1, 3, last=True)
        weighted_values(0, 3)

    @pl.when(qi % 2 == 0)
    def _():
        weighted_values(0, 1)

    o_ref[...] = acc_scr[...].astype(o_ref.dtype)


def sb_attention(q, k, v, *, batch, seq, heads, q_col0, k_col0, v_col0, tq, tk, tw, ts, q_scale):
    assert tq == 2 * tk and tk % tw == 0
    nq = seq // tq
    r = lax.broadcasted_iota(jnp.int32, (tw, tw), 0)
    c = lax.broadcasted_iota(jnp.int32, (tw, tw), 1)
    tri = (r > c).astype(BF16)
    return pl.pallas_call(
        functools.partial(_sb_kernel, tq=tq, tk=tk, tw=tw, ts=ts, q_scale=q_scale),
        out_shape=jax.ShapeDtypeStruct((batch * seq, heads * LANES), BF16),
        grid=(batch, heads, nq),
        in_specs=[
            pl.BlockSpec((tq, LANES), lambda b, h, i: (b * nq + i, q_col0 + h)),
            pl.BlockSpec((seq, LANES), lambda b, h, i: (b, k_col0 + h)),
            pl.BlockSpec((seq, LANES), lambda b, h, i: (b, v_col0 + h)),
            pl.BlockSpec((tw, tw), lambda b, h, i: (0, 0)),
        ],
        out_specs=pl.BlockSpec((tq, LANES), lambda b, h, i: (b * nq + i, h)),
        scratch_shapes=[pltpu.VMEM((4, tq, tk), F32), pltpu.VMEM((4, tq, tk), BF16),
                        pltpu.VMEM((tq, LANES), F32), pltpu.VMEM((tq, LANES), F32)],
        compiler_params=_params("parallel", "parallel", "arbitrary"),
        name="sb_attention",
    )(q, k, v, tri)


def _merge_kernel(x_ref, g_ref, wg_ref, oa_ref, ob_ref, oc_ref, lam_ref, subln_ref,
                  wba_ref, wbb_ref, wbc_ref, wo_ref, o_ref, *, lam_init):
    x = x_ref[...]
    d = x.shape[1]
    h = _rms(x, g_ref[...]).astype(BF16)
    lp = lam_ref[...]
    lam = (jnp.exp(jnp.sum(lp[0:1] * lp[1:2], axis=1, keepdims=True))
           - jnp.exp(jnp.sum(lp[2:3] * lp[3:4], axis=1, keepdims=True)) + lam_init)
    oc = oc_ref[...]
    parts = []
    for hh in range(DIFF_HEADS):
        p1 = oc[:, (2 * hh) * LANES:(2 * hh + 1) * LANES]
        p2 = oc[:, (2 * hh + 1) * LANES:(2 * hh + 2) * LANES]
        parts.append((_rms(p1 - lam * p2, subln_ref[...]) * (1.0 - lam_init)).astype(BF16))
    ocn = jnp.concatenate(parts, axis=1)

    def gate(i):
        return jax.nn.sigmoid(_dot(h, wg_ref[:, i * d:(i + 1) * d]))

    merged = gate(0) * _dot(oa_ref[...], wba_ref[...])
    merged += gate(1) * _dot(ob_ref[...], wbb_ref[...])
    merged += gate(2) * _dot(ocn, wbc_ref[...])
    o_ref[...] = x + _dot(merged.astype(BF16), wo_ref[...])


def merge(x, g, wg, oa, ob, oc, lam_p, subln, wba, wbb, wbc, wo, *, lam_init, tm):
    n, d = x.shape
    full = lambda a: pl.BlockSpec(a.shape, lambda i: (0,) * a.ndim)
    row = lambda a: pl.BlockSpec((tm, a.shape[1]), lambda i: (i, 0))
    g2 = g.reshape(1, d)
    subln2 = subln.reshape(1, -1)
    return pl.pallas_call(
        functools.partial(_merge_kernel, lam_init=lam_init),
        out_shape=jax.ShapeDtypeStruct((n, d), F32),
        grid=(n // tm,),
        in_specs=[row(x), full(g2), full(wg), row(oa), row(ob), row(oc), full(lam_p), full(subln2),
                  full(wba), full(wbb), full(wbc), full(wo)],
        out_specs=row(x),
        compiler_params=_params("parallel"),
        name="merge",
    )(x, g2, wg, oa, ob, oc, lam_p, subln2, wba, wbb, wbc, wo)


def _cross_kernel(x_ref, g_ref, wq_ref, k_ref, v_ref, wo_ref, o_ref, *, scale):
    x = x_ref[...]
    h = _rms(x, g_ref[...]).astype(BF16)
    q = (_dot(h, wq_ref[...]) * scale).astype(BF16)
    outs = []
    for hh in range(X_HEADS):
        sl = slice(hh * X_DIM, (hh + 1) * X_DIM)
        s = _dot_nt(q[:, sl], k_ref[:, sl])
        p = jnp.exp(s - jnp.max(s, axis=1, keepdims=True))
        o = _dot(p.astype(BF16), v_ref[:, sl]) / jnp.sum(p, axis=1, keepdims=True)
        outs.append(o.astype(BF16))
    o_ref[...] = x + _dot(jnp.concatenate(outs, axis=1), wo_ref[...])


def cross_attention(x, g, wq, kv, wo, *, batch, seq, tm):
    n, d = x.shape
    n_mem = kv.shape[0] // batch
    width = X_HEADS * X_DIM
    steps = seq // tm
    full = lambda a: pl.BlockSpec(a.shape, lambda b, i: (0,) * a.ndim)
    g2 = g.reshape(1, d)
    return pl.pallas_call(
        functools.partial(_cross_kernel, scale=X_DIM ** -0.5),
        out_shape=jax.ShapeDtypeStruct((n, d), F32),
        grid=(batch, steps),
        in_specs=[
            pl.BlockSpec((tm, d), lambda b, i: (b * steps + i, 0)),
            full(g2), full(wq),
            pl.BlockSpec((n_mem, width), lambda b, i: (b, 0)),
            pl.BlockSpec((n_mem, width), lambda b, i: (b, 1)),
            full(wo),
        ],
        out_specs=pl.BlockSpec((tm, d), lambda b, i: (b * steps + i, 0)),
        compiler_params=_params("parallel", "parallel"),
        name="cross_attention",
    )(x, g2, wq, kv, kv, wo)


def _router_kernel(x_ref, g_ref, w_ref, b_ref, comb_ref):
    h = _rms(x_ref[...], g_ref[...])
    logits = jnp.dot(h, w_ref[...], preferred_element_type=F32, precision=lax.Precision.HIGHEST) + b_ref[...]
    lane = lax.broadcasted_iota(jnp.int32, logits.shape, 1).astype(F32)
    neg = -jnp.inf
    far = float(4 * LANES)

    def first_argmax(vals, vmax):
        return jnp.min(jnp.where(vals == vmax, lane, far), axis=1, keepdims=True)

    gl = jnp.where((lane >= GROUP_LANE0) & (lane < GROUP_LANE0 + N_GROUPS), logits, neg)
    gmax = jnp.max(gl, axis=1, keepdims=True)
    lo = (first_argmax(gl, gmax) - GROUP_LANE0) * EXPERTS_PER_GROUP
    p_g = 1.0 / jnp.sum(jnp.exp(gl - gmax), axis=1, keepdims=True)
    el = jnp.where((lane >= lo) & (lane < lo + EXPERTS_PER_GROUP), logits, neg)
    v1 = jnp.max(el, axis=1, keepdims=True)
    i1 = first_argmax(el, v1)
    el2 = jnp.where(lane == i1, neg, el)
    v2 = jnp.max(el2, axis=1, keepdims=True)
    i2 = first_argmax(el2, v2)
    t = jnp.exp(v2 - v1)
    w1 = 1.0 / (1.0 + t)
    comb_ref[...] = jnp.where(lane == i1, w1 * p_g, 0.0) + jnp.where(lane == i2, (t * w1) * p_g, 0.0)


def router(x, g, w, b, *, tm):
    n, d = x.shape
    full = lambda a: pl.BlockSpec(a.shape, lambda i: (0,) * a.ndim)
    g2 = g.reshape(1, d)
    return pl.pallas_call(
        _router_kernel,
        out_shape=jax.ShapeDtypeStruct((n, LANES), F32),
        grid=(n // tm,),
        in_specs=[pl.BlockSpec((tm, d), lambda i: (i, 0)), full(g2), full(w), full(b)],
        out_specs=pl.BlockSpec((tm, LANES), lambda i: (i, 0)),
        compiler_params=_params("parallel"),
        name="router",
    )(x, g2, w, b)


def _moe_kernel(x_ref, g_ref, comb_ref, wg_ref, wu_ref, wd_ref, fg_ref, o_ref, h_scr, y_scr, *, ec, final_norm):
    j = pl.program_id(1)

    @pl.when(j == 0)
    def _():
        h_scr[...] = _rms(x_ref[...], g_ref[...]).astype(BF16)
        y_scr[...] = jnp.zeros(y_scr.shape, F32)

    h = h_scr[...]
    comb = comb_ref[...]
    lane = lax.broadcasted_iota(jnp.int32, comb.shape, 1)
    for e in range(ec):
        a = _dot(h, wg_ref[e])
        u = _dot(h, wu_ref[e])
        cw = jnp.sum(jnp.where(lane == j * ec + e, comb, 0.0), axis=1, keepdims=True)
        act = (a * jax.nn.sigmoid(a)) * u * cw
        y_scr[...] += _dot(act.astype(BF16), wd_ref[e])

    @pl.when(j == pl.num_programs(1) - 1)
    def _():
        y = x_ref[...] + y_scr[...]
        if final_norm:
            y = _rms(y, fg_ref[...])
        o_ref[...] = y


def moe(x, g, comb, wg, wu, wd, fg, *, tm, ec, final_norm):
    n, d = x.shape
    n_exp, _, f = wg.shape
    g2 = g.reshape(1, d)
    fg2 = fg.reshape(1, d)
    return pl.pallas_call(
        functools.partial(_moe_kernel, ec=ec, final_norm=final_norm),
        out_shape=jax.ShapeDtypeStruct((n, d), F32),
        grid=(n // tm, n_exp // ec),
        in_specs=[
            pl.BlockSpec((tm, d), lambda i, j: (i, 0)),
            pl.BlockSpec((1, d), lambda i, j: (0, 0)),
            pl.BlockSpec((tm, LANES), lambda i, j: (i, 0)),
            pl.BlockSpec((ec, d, f), lambda i, j: (j, 0, 0)),
            pl.BlockSpec((ec, d, f), lambda i, j: (j, 0, 0)),
            pl.BlockSpec((ec, f, d), lambda i, j: (j, 0, 0)),
            pl.BlockSpec((1, d), lambda i, j: (0, 0)),
        ],
        out_specs=pl.BlockSpec((tm, d), lambda i, j: (i, 0)),
        scratch_shapes=[pltpu.VMEM((tm, d), BF16), pltpu.VMEM((tm, d), F32)],
        compiler_params=_params("parallel", "arbitrary"),
        name="moe",
    )(x, g2, comb, wg, wu, wd, fg2)


def _pad_heads(w, heads, dim, lane0=0):
    k = w.shape[0]
    w = w.reshape(k, heads, dim)
    w = jnp.pad(w, ((0, 0), (0, 0), (lane0, LANES - dim - lane0)))
    return w.reshape(k, heads * LANES)


def _pad_head_rows(w, heads, dim):
    m = w.shape[1]
    w = w.reshape(heads, dim, m)
    w = jnp.pad(w, ((0, 0), (0, LANES - dim), (0, 0)))
    return w.reshape(heads * LANES, m)


def _swap_halves(w):
    half = w.shape[-1] // 2
    return jnp.concatenate([w[..., half:], w[..., :half]], axis=-1)


def _rope_tables(seq):
    half = MLA_ROPE // 2
    inv_freq = ROPE_THETA ** (-jnp.arange(half, dtype=F32) / half)
    ang = jnp.arange(seq, dtype=F32)[:, None] * inv_freq[None, :]
    cos, sin = jnp.cos(ang), jnp.sin(ang)
    ones = jnp.ones((seq, MLA_NOPE), F32)
    zn = jnp.zeros((seq, MLA_NOPE), F32)
    zp = jnp.zeros((seq, LANES - MLA_NOPE - MLA_ROPE), F32)
    return (jnp.concatenate([ones, cos, cos, zp], axis=1),
            jnp.concatenate([zn, -sin, sin, zp], axis=1))


def kernel(x, mem, attn_norm, w_in, q_norm, w_uq, kv_norm, w_ukv, diff_lambda, diff_subln, w_branch, w_out,
           cross_norm, mem_norm, w_xq, w_xkv, w_xo, ffn_norm, w_router_grp, b_router_grp, w_router_exp,
           b_router_exp, w_exp_gate, w_exp_up, w_exp_down, final_norm):
    batch, seq, d = x.shape
    depth = w_in.shape[0]
    n = batch * seq
    n_mem = mem.shape[1]
    tm = min(512, seq)
    tm_big = min(1024, seq)
    tq_attn = min(1024, seq)
    tk_attn = tq_attn // 2
    ts_attn = min(64, seq)
    ts_sb = min(256, seq)

    gate_cols = 3 * d
    o_cq = gate_cols
    o_ckv = o_cq + MLA_Q_RANK
    o_kr = o_ckv + MLA_KV_RANK
    o_sb = o_kr + MLA_ROPE
    sb_w = SB_HEADS * SB_DIM
    o_diff = o_sb + 3 * sb_w
    diff_qk = DIFF_HEADS * 2 * DIFF_DIM

    cos_t, sin_t = _rope_tables(seq)
    xf = x.reshape(n, d)
    mem_f = mem.reshape(batch * n_mem, d)

    for l in range(depth):
        wl = w_in[l]
        kr_w = wl[:, o_kr:o_sb]
        w1 = jnp.concatenate([
            wl[:, o_cq:o_kr],
            jnp.pad(kr_w, ((0, 0), (MLA_NOPE, LANES - MLA_NOPE - MLA_ROPE))),
            jnp.pad(_swap_halves(kr_w), ((0, 0), (MLA_NOPE, LANES - MLA_NOPE - MLA_ROPE))),
        ], axis=1).astype(BF16)
        uq = w_uq[l].reshape(MLA_Q_RANK, MLA_HEADS, MLA_NOPE + MLA_ROPE)
        uq_nope, uq_rope = uq[..., :MLA_NOPE], uq[..., MLA_NOPE:]
        padq = ((0, 0), (0, 0), (0, LANES - MLA_NOPE - MLA_ROPE))
        wq = jnp.pad(jnp.concatenate([uq_nope, uq_rope], -1), padq).reshape(MLA_Q_RANK, -1).astype(BF16)
        wqs = jnp.pad(jnp.concatenate([jnp.zeros_like(uq_nope), _swap_halves(uq_rope)], -1), padq)
        wqs = wqs.reshape(MLA_Q_RANK, -1).astype(BF16)
        ukv = w_ukv[l].reshape(MLA_KV_RANK, MLA_HEADS, MLA_NOPE + MLA_V)
        wk = _pad_heads(ukv[..., :MLA_NOPE].reshape(MLA_KV_RANK, -1), MLA_HEADS, MLA_NOPE).astype(BF16)
        wv = _pad_heads(ukv[..., MLA_NOPE:].reshape(MLA_KV_RANK, -1), MLA_HEADS, MLA_V).astype(BF16)
        q_a, k_a, v_a = mla_prep(xf, attn_norm[l], w1, q_norm[l], kv_norm[l], wq, wqs, wk, wv,
                                 cos_t, sin_t, tm=tm, seq=seq)

        sb = wl[:, o_sb:o_diff]
        dq = wl[:, o_diff:o_diff + diff_qk].reshape(d, DIFF_HEADS, 2, DIFF_DIM)
        dq = jnp.stack([jnp.pad(dq[:, :, 0], ((0, 0), (0, 0), (0, DIFF_DIM))),
                        jnp.pad(dq[:, :, 1], ((0, 0), (0, 0), (DIFF_DIM, 0)))], axis=2)
        w2 = jnp.concatenate([
            _pad_heads(sb[:, :sb_w], SB_HEADS, SB_DIM),
            _pad_heads(sb[:, sb_w:2 * sb_w], SB_HEADS, SB_DIM),
            _pad_heads(sb[:, 2 * sb_w:], SB_HEADS, SB_DIM),
            dq.reshape(d, 2 * DIFF_HEADS * LANES),
            wl[:, o_diff + diff_qk:o_diff + 3 * diff_qk],
        ], axis=1).astype(BF16)
        p2 = norm_matmul(xf, attn_norm[l], w2, tm=tm_big, tn=w2.shape[1] // 2)
        cb_sb_q, cb_sb_k, cb_sb_v = 0, SB_HEADS, 2 * SB_HEADS
        cb_d_q = 3 * SB_HEADS
        cb_d_k = cb_d_q + 2 * DIFF_HEADS
        cb_d_v = cb_d_k + DIFF_HEADS

        o_a = flash_attention(q_a, k_a, v_a, batch=batch, seq=seq, heads=MLA_HEADS, q_col0=0, k_col0=0,
                              v_col0=0, heads_per_kv=1, tq=tq_attn, tk=tk_attn, ts=ts_attn)
        o_b = sb_attention(p2, p2, p2, batch=batch, seq=seq, heads=SB_HEADS, q_col0=cb_sb_q,
                           k_col0=cb_sb_k, v_col0=cb_sb_v, tq=tq_attn, tk=tk_attn, tw=min(256, tk_attn), ts=ts_sb,
                           q_scale=SB_DIM ** -0.5 * LOG2E)
        slopes = tuple(2.0 ** (-8.0 * (i + 1) / DIFF_HEADS) for i in range(DIFF_HEADS))
        o_c = flash_attention(p2, p2, p2, batch=batch, seq=seq, heads=2 * DIFF_HEADS, q_col0=cb_d_q,
                              k_col0=cb_d_k, v_col0=cb_d_v, heads_per_kv=2, tq=tq_attn, tk=tk_attn, ts=ts_attn,
                              q_scale=DIFF_DIM ** -0.5 * LOG2E, slopes=slopes, heads_per_slope=2, out_dtype=F32)

        lam_init = 0.8 - 0.6 * math.exp(-0.3 * l)
        wba = _pad_head_rows(w_branch[l, 0], MLA_HEADS, MLA_V).astype(BF16)
        wbb = _pad_head_rows(w_branch[l, 1], SB_HEADS, SB_DIM).astype(BF16)
        xf = merge(xf, attn_norm[l], wl[:, :gate_cols].astype(BF16), o_a, o_b, o_c, diff_lambda[l],
                   diff_subln[l], wba, wbb, w_branch[l, 2].astype(BF16), w_out[l].astype(BF16),
                   lam_init=lam_init, tm=min(256, seq))

        kv = norm_matmul(mem_f, mem_norm, w_xkv[l].astype(BF16), tm=min(256, n_mem), tn=512)
        xf = cross_attention(xf, cross_norm[l], w_xq[l].astype(BF16), kv, w_xo[l].astype(BF16),
                             batch=batch, seq=seq, tm=tm)

        pad_r = LANES - N_EXPERTS - N_GROUPS
        wr = jnp.pad(jnp.concatenate([w_router_exp[l], w_router_grp[l]], axis=1), ((0, 0), (0, pad_r)))
        br = jnp.pad(jnp.concatenate([b_router_exp[l], b_router_grp[l]]), (0, pad_r)).reshape(1, LANES)
        comb = router(xf, ffn_norm[l], wr, br, tm=tm)
        xf = moe(xf, ffn_norm[l], comb, w_exp_gate[l].astype(BF16), w_exp_up[l].astype(BF16),
                 w_exp_down[l].astype(BF16), final_norm, tm=tm_big, ec=EXPERTS_PER_GROUP,
                 final_norm=(l == depth - 1))

    return xf.reshape(batch, seq, d)
```

```python
import functools
import math

import jax
import jax.numpy as jnp
from jax import lax
from jax.experimental import pallas as pl
from jax.experimental.pallas import tpu as pltpu

F32 = jnp.float32
BF16 = jnp.bfloat16
EPS = 1e-6
LANES = 128
LOG2E = math.log2(math.e)
VMEM_LIMIT_BYTES = 56 * 1024 * 1024

ROPE_THETA = 10000.0
MLA_HEADS, MLA_NOPE, MLA_ROPE, MLA_V = 8, 64, 32, 64
MLA_Q_RANK, MLA_KV_RANK = 256, 256
SB_HEADS, SB_DIM = 8, 64
DIFF_HEADS, DIFF_DIM = 4, 64
X_HEADS, X_DIM = 4, 128
N_GROUPS, EXPERTS_PER_GROUP = 4, 8
N_EXPERTS = N_GROUPS * EXPERTS_PER_GROUP
GROUP_LANE0 = N_EXPERTS


def _params(*semantics):
    return pltpu.CompilerParams(dimension_semantics=semantics, vmem_limit_bytes=VMEM_LIMIT_BYTES)


def _rms(x, g):
    return x * lax.rsqrt(jnp.mean(x * x, axis=-1, keepdims=True) + EPS) * g


def _dot(a, b):
    return jnp.dot(a, b, preferred_element_type=F32)


def _dot_nt(a, b):
    return lax.dot_general(a, b, (((1,), (1,)), ((), ())), preferred_element_type=F32)


def _norm_matmul_kernel(x_ref, g_ref, w_ref, o_ref, h_scr):
    @pl.when(pl.program_id(1) == 0)
    def _():
        h_scr[...] = _rms(x_ref[...], g_ref[...]).astype(BF16)

    o_ref[...] = _dot(h_scr[...], w_ref[...]).astype(o_ref.dtype)


def norm_matmul(x, g, w, *, tm, tn, out_dtype=BF16):
    n, k = x.shape
    m = w.shape[1]
    return pl.pallas_call(
        _norm_matmul_kernel,
        out_shape=jax.ShapeDtypeStruct((n, m), out_dtype),
        grid=(n // tm, m // tn),
        in_specs=[
            pl.BlockSpec((tm, k), lambda i, j: (i, 0)),
            pl.BlockSpec((1, k), lambda i, j: (0, 0)),
            pl.BlockSpec((k, tn), lambda i, j: (0, j)),
        ],
        out_specs=pl.BlockSpec((tm, tn), lambda i, j: (i, j)),
        scratch_shapes=[pltpu.VMEM((tm, k), BF16)],
        compiler_params=_params("parallel", "arbitrary"),
        name="norm_matmul",
    )(x, g.reshape(1, k), w)


def _mla_prep_kernel(x_ref, g_ref, w1_ref, qn_ref, kvn_ref, wq_ref, wqs_ref, wk_ref, wv_ref,
                     cos_ref, sin_ref, q_ref, k_ref, v_ref, *, scale):
    h = _rms(x_ref[...], g_ref[...]).astype(BF16)
    c = _dot(h, w1_ref[...])
    cq = _rms(c[:, :MLA_Q_RANK], qn_ref[...]).astype(BF16)
    ckv = _rms(c[:, MLA_Q_RANK:MLA_Q_RANK + MLA_KV_RANK], kvn_ref[...]).astype(BF16)
    base = MLA_Q_RANK + MLA_KV_RANK
    cos = cos_ref[...]
    sin = sin_ref[...]
    kr = c[:, base:base + LANES] * cos + c[:, base + LANES:base + 2 * LANES] * sin
    q = _dot(cq, wq_ref[...])
    qs = _dot(cq, wqs_ref[...])
    k = _dot(ckv, wk_ref[...])
    v_ref[...] = _dot(ckv, wv_ref[...]).astype(BF16)
    for hh in range(MLA_HEADS):
        sl = slice(hh * LANES, (hh + 1) * LANES)
        q_ref[:, sl] = ((q[:, sl] * cos + qs[:, sl] * sin) * scale).astype(BF16)
        k_ref[:, sl] = (k[:, sl] + kr).astype(BF16)


def mla_prep(x, g, w1, qn, kvn, wq, wqs, wk, wv, cos_t, sin_t, *, tm, seq):
    n, d = x.shape
    width = MLA_HEADS * LANES
    steps_per_seq = seq // tm
    full = lambda a: pl.BlockSpec(a.shape, lambda i: (0,) * a.ndim)
    out = jax.ShapeDtypeStruct((n, width), BF16)
    row = pl.BlockSpec((tm, width), lambda i: (i, 0))
    args = (g.reshape(1, d), w1, qn.reshape(1, -1), kvn.reshape(1, -1), wq, wqs, wk, wv)
    return pl.pallas_call(
        functools.partial(_mla_prep_kernel, scale=(MLA_NOPE + MLA_ROPE) ** -0.5 * LOG2E),
        out_shape=(out, out, out),
        grid=(n // tm,),
        in_specs=[pl.BlockSpec((tm, d), lambda i: (i, 0))] + [full(a) for a in args] + [
            pl.BlockSpec((tm, LANES), lambda i: (i % steps_per_seq, 0)),
            pl.BlockSpec((tm, LANES), lambda i: (i % steps_per_seq, 0)),
        ],
        out_specs=(row, row, row),
        compiler_params=_params("parallel"),
        name="mla_prep",
    )(x, *args, cos_t, sin_t)


def _flash_kernel(q_ref, k_ref, v_ref, o_ref, s_scr, p_scr, alpha_scr, m_scr, l_scr, acc_scr, *, tq, tk, ts,
                  q_scale, slopes, heads_per_slope):
    qi = pl.program_id(2)
    nchunk = tk // LANES
    n_full = 2 * qi
    m_scr[...] = jnp.full(m_scr.shape, -jnp.inf, F32)
    l_scr[...] = jnp.zeros(l_scr.shape, F32)
    acc_scr[...] = jnp.zeros(acc_scr.shape, F32)
    q = q_ref[...]
    if q_scale != 1.0:
        q = (q.astype(F32) * q_scale).astype(BF16)

    if slopes is not None:
        slope_idx = pl.program_id(1) // heads_per_slope
        slope = jnp.float32(slopes[-1])
        for i in range(len(slopes) - 2, -1, -1):
            slope = jnp.where(slope_idx == i, jnp.float32(slopes[i]), slope)
        col = lax.broadcasted_iota(jnp.int32, (1, tk), 1)
    lane_minus_row = (lax.broadcasted_iota(jnp.int32, (ts, LANES), 1)
                      - lax.broadcasted_iota(jnp.int32, (ts, LANES), 0))

    def scores(kb, slot):
        s = _dot_nt(q, k_ref[pl.ds(pl.multiple_of(kb * tk, tk), tk), :])
        if slopes is not None:
            s = s + (slope * LOG2E) * (col + (kb * tk - qi * tq)).astype(F32)
        s_scr[slot] = s

    def weighted_values(kb, slot):
        pv = _dot(p_scr[slot], v_ref[pl.ds(pl.multiple_of(kb * tk, tk), tk), :])
        acc_scr[...] = alpha_scr[slot] * acc_scr[...] + pv

    def softmax(slot, local):
        for u in range(tq // ts):
            rows = slice(u * ts, (u + 1) * ts)
            last = u * ts + ts - 1
            live = [c for c in range(nchunk) if local is None or local + c * LANES <= last]
            if not live:
                p_scr[slot, rows, :] = jnp.zeros((ts, tk), BF16)
                alpha_scr[slot, rows, :] = jnp.ones((ts, LANES), F32)
                continue
            sc = [s_scr[slot, rows, c * LANES:(c + 1) * LANES] for c in live]
            if local is not None:
                for i, c in enumerate(live):
                    if local + c * LANES + LANES - 1 > u * ts:
                        sc[i] = jnp.where(lane_minus_row <= u * ts - local - c * LANES, sc[i], -jnp.inf)
            m_prev = m_scr[rows, :]
            m_new = jnp.maximum(m_prev, jnp.max(functools.reduce(jnp.maximum, sc), axis=1, keepdims=True))
            alpha = jnp.exp2(m_prev - m_new)
            ps = [jnp.exp2(x - m_new) for x in sc]
            l_scr[rows, :] = alpha * l_scr[rows, :] + jnp.sum(functools.reduce(jnp.add, ps), axis=1, keepdims=True)
            for c, pc in zip(live, ps):
                p_scr[slot, rows, c * LANES:(c + 1) * LANES] = pc.astype(BF16)
            if len(live) < nchunk:
                p_scr[slot, rows, len(live) * LANES:] = jnp.zeros((ts, tk - len(live) * LANES), BF16)
            alpha_scr[slot, rows, :] = alpha
            m_scr[rows, :] = m_new

    last_full = jnp.maximum(n_full - 1, 0)
    scores(n_full, 0)
    scores(n_full + 1, 1)
    softmax(0, 0)
    weighted_values(n_full, 0)
    softmax(1, tk)
    scores(0, 2)

    def trip(j, slot, last=False):
        weighted_values(jnp.where(j == 0, n_full + 1, j - 1), (slot - 1) % 4)
        softmax(slot, None)
        if not last:
            scores(jnp.minimum(j + 1, last_full), (slot + 1) % 4)

    def quad(i, carry):
        trip(4 * i, 2)
        trip(4 * i + 1, 3)
        trip(4 * i + 2, 0)
        trip(4 * i + 3, 1)
        return carry

    lax.fori_loop(0, qi // 2, quad, 0)

    @pl.when(qi % 2 == 1)
    def _():
        trip(n_full - 2, 2)
        trip(n_full - 1, 3, last=True)
        weighted_values(n_full - 1, 3)

    @pl.when(qi % 2 == 0)
    def _():
        weighted_values(jnp.where(qi == 0, 1, n_full - 1), 1)

    o_ref[...] = (acc_scr[...] / l_scr[...]).astype(o_ref.dtype)


def flash_attention(q, k, v, *, batch, seq, heads, q_col0, k_col0, v_col0, heads_per_kv, tq, tk, ts,
                    q_scale=1.0, slopes=None, heads_per_slope=1, out_dtype=BF16):
    nq = seq // tq
    assert tq == 2 * tk
    kernel = functools.partial(_flash_kernel, tq=tq, tk=tk, ts=ts, q_scale=q_scale, slopes=slopes,
                               heads_per_slope=heads_per_slope)
    return pl.pallas_call(
        kernel,
        out_shape=jax.ShapeDtypeStruct((batch * seq, heads * LANES), out_dtype),
        grid=(batch, heads, nq),
        in_specs=[
            pl.BlockSpec((tq, LANES), lambda b, h, i: (b * nq + i, q_col0 + h)),
            pl.BlockSpec((seq, LANES), lambda b, h, i: (b, k_col0 + h // heads_per_kv)),
            pl.BlockSpec((seq, LANES), lambda b, h, i: (b, v_col0 + h // heads_per_kv)),
        ],
        out_specs=pl.BlockSpec((tq, LANES), lambda b, h, i: (b * nq + i, h)),
        scratch_shapes=[pltpu.VMEM((4, tq, tk), F32), pltpu.VMEM((4, tq, tk), BF16),
                        pltpu.VMEM((4, tq, LANES), F32)] + [pltpu.VMEM((tq, LANES), F32)] * 3,
        compiler_params=_params("parallel", "parallel", "arbitrary"),
        name="flash_attention",
    )(q, k, v)


def _sb_kernel(q_ref, k_ref, v_ref, tri_ref, o_ref, z_scr, p_scr, c_scr, acc_scr, *, tq, tk, tw, ts, q_scale):
    qi = pl.program_id(2)
    nchunk = tk // LANES
    q = (q_ref[...].astype(F32) * q_scale).astype(BF16)
    tri = tri_ref[...]
    c_scr[...] = jnp.zeros(c_scr.shape, F32)
    acc_scr[...] = jnp.zeros(acc_scr.shape, F32)
    lane_minus_row = (lax.broadcasted_iota(jnp.int32, (ts, LANES), 1)
                      - lax.broadcasted_iota(jnp.int32, (ts, LANES), 0))

    def scores(kc, slot):
        z_scr[slot] = _dot_nt(q, k_ref[pl.ds(pl.multiple_of(kc * tk, tk), tk), :])

    def weighted_values(kc, slot):
        acc_scr[...] += _dot(p_scr[slot], v_ref[pl.ds(pl.multiple_of(kc * tk, tk), tk), :])

    def weights(slot, local):
        for u in range(tq // ts):
            rows = slice(u * ts, (u + 1) * ts)
            def chunk_state(c):
                if local is None or local + c * LANES + LANES - 1 < u * ts:
                    return "full"
                return "dead" if local + c * LANES >= u * ts + ts - 1 else "masked"

            state = [chunk_state(c) for c in range(nchunk)]
            lss, lks, valid = {}, {}, {}
            for c in range(nchunk):
                if state[c] == "dead":
                    continue
                z = z_scr[slot, rows, c * LANES:(c + 1) * LANES]
                lss[c] = jnp.minimum(z, 0.0) - jnp.log2(1.0 + jnp.exp2(-jnp.abs(z)))
                lks[c] = lss[c] - z
                if state[c] == "masked":
                    valid[c] = lane_minus_row < u * ts - local - c * LANES
                    lks[c] = jnp.where(valid[c], lks[c], 0.0)
            later = c_scr[rows, :]
            gc = tw // LANES
            for g in range(nchunk // gc - 1, -1, -1):
                chunks = range(g * gc, (g + 1) * gc)
                live = [c for c in chunks if state[c] != "dead"]
                for c in chunks:
                    if state[c] == "dead":
                        p_scr[slot, rows, c * LANES:(c + 1) * LANES] = jnp.zeros((ts, LANES), BF16)
                if not live:
                    continue
                grp = [lks[c] if c in lks else jnp.zeros((ts, LANES), F32) for c in chunks]
                after = jnp.concatenate([later] * gc, axis=1) + _dot(jnp.concatenate(grp, axis=1).astype(BF16), tri)
                for c in live:
                    i = c - g * gc
                    p = jnp.exp2(lss[c] + after[:, i * LANES:(i + 1) * LANES])
                    if state[c] == "masked":
                        p = jnp.where(valid[c], p, 0.0)
                    p_scr[slot, rows, c * LANES:(c + 1) * LANES] = p.astype(BF16)
                later = later + jnp.sum(functools.reduce(jnp.add, [lks[c] for c in live]), axis=1, keepdims=True)
            if lks:
                c_scr[rows, :] = later

    n = 2 * (qi + 1)
    scores(n - 1, 0)
    weights(0, tk)
    scores(n - 2, 1)
    weighted_values(n - 1, 0)
    weights(1, 0)
    scores(jnp.maximum(n - 3, 0), 2)

    def trip(t, slot, last=False):
        kc = n - 1 - t
        weighted_values(kc + 1, (slot - 1) % 4)
        weights(slot, None)
        if not last:
            scores(jnp.maximum(kc - 1, 0), (slot + 1) % 4)

    def quad(i, carry):
        t = 2 + 4 * i
        trip(t, 2)
        trip(t + 1, 3)
        trip(t + 2, 0)
        trip(t + 3, 1)
        return carry

    lax.fori_loop(0, qi // 2, quad, 0)

    @pl.when(qi % 2 == 1)
    def _():
        t = 2 + 4 * (qi // 2)
        trip(t, 2)
        trip(t + 1, 3, last=True)
        weighted_values(0, 3)

    @pl.when(qi % 2 == 0)
    def _():
        weighted_values(0, 1)

    o_ref[...] = acc_scr[...].astype(o_ref.dtype)


def sb_attention(q, k, v, *, batch, seq, heads, q_col0, k_col0, v_col0, tq, tk, tw, ts, q_scale):
    assert tq == 2 * tk and tk % tw == 0
    nq = seq // tq
    r = lax.broadcasted_iota(jnp.int32, (tw, tw), 0)
    c = lax.broadcasted_iota(jnp.int32, (tw, tw), 1)
    tri = (r > c).astype(BF16)
    return pl.pallas_call(
        functools.partial(_sb_kernel, tq=tq, tk=tk, tw=tw, ts=ts, q_scale=q_scale),
        out_shape=jax.ShapeDtypeStruct((batch * seq, heads * LANES), BF16),
        grid=(batch, heads, nq),
        in_specs=[
            pl.BlockSpec((tq, LANES), lambda b, h, i: (b * nq + i, q_col0 + h)),
            pl.BlockSpec((seq, LANES), lambda b, h, i: (b, k_col0 + h)),
            pl.BlockSpec((seq, LANES), lambda b, h, i: (b, v_col0 + h)),
            pl.BlockSpec((tw, tw), lambda b, h, i: (0, 0)),
        ],
        out_specs=pl.BlockSpec((tq, LANES), lambda b, h, i: (b * nq + i, h)),
        scratch_shapes=[pltpu.VMEM((4, tq, tk), F32), pltpu.VMEM((4, tq, tk), BF16),
                        pltpu.VMEM((tq, LANES), F32), pltpu.VMEM((tq, LANES), F32)],
        compiler_params=_params("parallel", "parallel", "arbitrary"),
        name="sb_attention",
    )(q, k, v, tri)


def _merge_kernel(x_ref, g_ref, wg_ref, oa_ref, ob_ref, oc_ref, lam_ref, subln_ref,
                  wba_ref, wbb_ref, wbc_ref, wo_ref, o_ref, *, lam_init):
    x = x_ref[...]
    d = x.shape[1]
    h = _rms(x, g_ref[...]).astype(BF16)
    lp = lam_ref[...]
    lam = (jnp.exp(jnp.sum(lp[0:1] * lp[1:2], axis=1, keepdims=True))
           - jnp.exp(jnp.sum(lp[2:3] * lp[3:4], axis=1, keepdims=True)) + lam_init)
    oc = oc_ref[...]
    parts = []
    for hh in range(DIFF_HEADS):
        p1 = oc[:, (2 * hh) * LANES:(2 * hh + 1) * LANES]
        p2 = oc[:, (2 * hh + 1) * LANES:(2 * hh + 2) * LANES]
        parts.append((_rms(p1 - lam * p2, subln_ref[...]) * (1.0 - lam_init)).astype(BF16))
    ocn = jnp.concatenate(parts, axis=1)

    def gate(i):
        return jax.nn.sigmoid(_dot(h, wg_ref[:, i * d:(i + 1) * d]))

    merged = gate(0) * _dot(oa_ref[...], wba_ref[...])
    merged += gate(1) * _dot(ob_ref[...], wbb_ref[...])
    merged += gate(2) * _dot(ocn, wbc_ref[...])
    o_ref[...] = x + _dot(merged.astype(BF16), wo_ref[...])


def merge(x, g, wg, oa, ob, oc, lam_p, subln, wba, wbb, wbc, wo, *, lam_init, tm):
    n, d = x.shape
    full = lambda a: pl.BlockSpec(a.shape, lambda i: (0,) * a.ndim)
    row = lambda a: pl.BlockSpec((tm, a.shape[1]), lambda i: (i, 0))
    g2 = g.reshape(1, d)
    subln2 = subln.reshape(1, -1)
    return pl.pallas_call(
        functools.partial(_merge_kernel, lam_init=lam_init),
        out_shape=jax.ShapeDtypeStruct((n, d), F32),
        grid=(n // tm,),
        in_specs=[row(x), full(g2), full(wg), row(oa), row(ob), row(oc), full(lam_p), full(subln2),
                  full(wba), full(wbb), full(wbc), full(wo)],
        out_specs=row(x),
        compiler_params=_params("parallel"),
        name="merge",
    )(x, g2, wg, oa, ob, oc, lam_p, subln2, wba, wbb, wbc, wo)


def _cross_kernel(x_ref, g_ref, wq_ref, k_ref, v_ref, wo_ref, o_ref, *, scale):
    x = x_ref[...]
    h = _rms(x, g_ref[...]).astype(BF16)
    q = (_dot(h, wq_ref[...]) * scale).astype(BF16)
    outs = []
    for hh in range(X_HEADS):
        sl = slice(hh * X_DIM, (hh + 1) * X_DIM)
        s = _dot_nt(q[:, sl], k_ref[:, sl])
        p = jnp.exp(s - jnp.max(s, axis=1, keepdims=True))
        o = _dot(p.astype(BF16), v_ref[:, sl]) / jnp.sum(p, axis=1, keepdims=True)
        outs.append(o.astype(BF16))
    o_ref[...] = x + _dot(jnp.concatenate(outs, axis=1), wo_ref[...])


def cross_attention(x, g, wq, kv, wo, *, batch, seq, tm):
    n, d = x.shape
    n_mem = kv.shape[0] // batch
    width = X_HEADS * X_DIM
    steps = seq // tm
    full = lambda a: pl.BlockSpec(a.shape, lambda b, i: (0,) * a.ndim)
    g2 = g.reshape(1, d)
    return pl.pallas_call(
        functools.partial(_cross_kernel, scale=X_DIM ** -0.5),
        out_shape=jax.ShapeDtypeStruct((n, d), F32),
        grid=(batch, steps),
        in_specs=[
            pl.BlockSpec((tm, d), lambda b, i: (b * steps + i, 0)),
            full(g2), full(wq),
            pl.BlockSpec((n_mem, width), lambda b, i: (b, 0)),
            pl.BlockSpec((n_mem, width), lambda b, i: (b, 1)),
            full(wo),
        ],
        out_specs=pl.BlockSpec((tm, d), lambda b, i: (b * steps + i, 0)),
        compiler_params=_params("parallel", "parallel"),
        name="cross_attention",
    )(x, g2, wq, kv, kv, wo)


def _router_kernel(x_ref, g_ref, w_ref, b_ref, comb_ref):
    h = _rms(x_ref[...], g_ref[...])
    logits = jnp.dot(h, w_ref[...], preferred_element_type=F32, precision=lax.Precision.HIGHEST) + b_ref[...]
    lane = lax.broadcasted_iota(jnp.int32, logits.shape, 1).astype(F32)
    neg = -jnp.inf
    far = float(4 * LANES)

    def first_argmax(vals, vmax):
        return jnp.min(jnp.where(vals == vmax, lane, far), axis=1, keepdims=True)

    gl = jnp.where((lane >= GROUP_LANE0) & (lane < GROUP_LANE0 + N_GROUPS), logits, neg)
    gmax = jnp.max(gl, axis=1, keepdims=True)
    lo = (first_argmax(gl, gmax) - GROUP_LANE0) * EXPERTS_PER_GROUP
    p_g = 1.0 / jnp.sum(jnp.exp(gl - gmax), axis=1, keepdims=True)
    el = jnp.where((lane >= lo) & (lane < lo + EXPERTS_PER_GROUP), logits, neg)
    v1 = jnp.max(el, axis=1, keepdims=True)
    i1 = first_argmax(el, v1)
    el2 = jnp.where(lane == i1, neg, el)
    v2 = jnp.max(el2, axis=1, keepdims=True)
    i2 = first_argmax(el2, v2)
    t = jnp.exp(v2 - v1)
    w1 = 1.0 / (1.0 + t)
    comb_ref[...] = jnp.where(lane == i1, w1 * p_g, 0.0) + jnp.where(lane == i2, (t * w1) * p_g, 0.0)


def router(x, g, w, b, *, tm):
    n, d = x.shape
    full = lambda a: pl.BlockSpec(a.shape, lambda i: (0,) * a.ndim)
    g2 = g.reshape(1, d)
    return pl.pallas_call(
        _router_kernel,
        out_shape=jax.ShapeDtypeStruct((n, LANES), F32),
        grid=(n // tm,),
        in_specs=[pl.BlockSpec((tm, d), lambda i: (i, 0)), full(g2), full(w), full(b)],
        out_specs=pl.BlockSpec((tm, LANES), lambda i: (i, 0)),
        compiler_params=_params("parallel"),
        name="router",
    )(x, g2, w, b)


def _moe_kernel(x_ref, g_ref, comb_ref, wg_ref, wu_ref, wd_ref, fg_ref, o_ref, h_scr, y_scr, *, ec, final_norm):
    j = pl.program_id(1)

    @pl.when(j == 0)
    def _():
        h_scr[...] = _rms(x_ref[...], g_ref[...]).astype(BF16)
        y_scr[...] = jnp.zeros(y_scr.shape, F32)

    h = h_scr[...]
    comb = comb_ref[...]
    lane = lax.broadcasted_iota(jnp.int32, comb.shape, 1)
    for e in range(ec):
        a = _dot(h, wg_ref[e])
        u = _dot(h, wu_ref[e])
        cw = jnp.sum(jnp.where(lane == j * ec + e, comb, 0.0), axis=1, keepdims=True)
        act = (a * jax.nn.sigmoid(a)) * u * cw
        y_scr[...] += _dot(act.astype(BF16), wd_ref[e])

    @pl.when(j == pl.num_programs(1) - 1)
    def _():
        y = x_ref[...] + y_scr[...]
        if final_norm:
            y = _rms(y, fg_ref[...])
        o_ref[...] = y


def moe(x, g, comb, wg, wu, wd, fg, *, tm, ec, final_norm):
    n, d = x.shape
    n_exp, _, f = wg.shape
    g2 = g.reshape(1, d)
    fg2 = fg.reshape(1, d)
    return pl.pallas_call(
        functools.partial(_moe_kernel, ec=ec, final_norm=final_norm),
        out_shape=jax.ShapeDtypeStruct((n, d), F32),
        grid=(n // tm, n_exp // ec),
        in_specs=[
            pl.BlockSpec((tm, d), lambda i, j: (i, 0)),
            pl.BlockSpec((1, d), lambda i, j: (0, 0)),
            pl.BlockSpec((tm, LANES), lambda i, j: (i, 0)),
            pl.BlockSpec((ec, d, f), lambda i, j: (j, 0, 0)),
            pl.BlockSpec((ec, d, f), lambda i, j: (j, 0, 0)),
            pl.BlockSpec((ec, f, d), lambda i, j: (j, 0, 0)),
            pl.BlockSpec((1, d), lambda i, j: (0, 0)),
        ],
        out_specs=pl.BlockSpec((tm, d), lambda i, j: (i, 0)),
        scratch_shapes=[pltpu.VMEM((tm, d), BF16), pltpu.VMEM((tm, d), F32)],
        compiler_params=_params("parallel", "arbitrary"),
        name="moe",
    )(x, g2, comb, wg, wu, wd, fg2)


def _pad_heads(w, heads, dim, lane0=0):
    k = w.shape[0]
    w = w.reshape(k, heads, dim)
    w = jnp.pad(w, ((0, 0), (0, 0), (lane0, LANES - dim - lane0)))
    return w.reshape(k, heads * LANES)


def _pad_head_rows(w, heads, dim):
    m = w.shape[1]
    w = w.reshape(heads, dim, m)
    w = jnp.pad(w, ((0, 0), (0, LANES - dim), (0, 0)))
    return w.reshape(heads * LANES, m)


def _swap_halves(w):
    half = w.shape[-1] // 2
    return jnp.concatenate([w[..., half:], w[..., :half]], axis=-1)


def _rope_tables(seq):
    half = MLA_ROPE // 2
    inv_freq = ROPE_THETA ** (-jnp.arange(half, dtype=F32) / half)
    ang = jnp.arange(seq, dtype=F32)[:, None] * inv_freq[None, :]
    cos, sin = jnp.cos(ang), jnp.sin(ang)
    ones = jnp.ones((seq, MLA_NOPE), F32)
    zn = jnp.zeros((seq, MLA_NOPE), F32)
    zp = jnp.zeros((seq, LANES - MLA_NOPE - MLA_ROPE), F32)
    return (jnp.concatenate([ones, cos, cos, zp], axis=1),
            jnp.concatenate([zn, -sin, sin, zp], axis=1))


def kernel(x, mem, attn_norm, w_in, q_norm, w_uq, kv_norm, w_ukv, diff_lambda, diff_subln, w_branch, w_out,
           cross_norm, mem_norm, w_xq, w_xkv, w_xo, ffn_norm, w_router_grp, b_router_grp, w_router_exp,
           b_router_exp, w_exp_gate, w_exp_up, w_exp_down, final_norm):
    batch, seq, d = x.shape
    depth = w_in.shape[0]
    n = batch * seq
    n_mem = mem.shape[1]
    tm = min(1024, seq)
    tq_attn = min(1024, seq)
    tk_attn = tq_attn // 2
    ts_attn = min(64, seq)
    ts_sb = min(256, seq)

    gate_cols = 3 * d
    o_cq = gate_cols
    o_ckv = o_cq + MLA_Q_RANK
    o_kr = o_ckv + MLA_KV_RANK
    o_sb = o_kr + MLA_ROPE
    sb_w = SB_HEADS * SB_DIM
    o_diff = o_sb + 3 * sb_w
    diff_qk = DIFF_HEADS * 2 * DIFF_DIM

    cos_t, sin_t = _rope_tables(seq)
    xf = x.reshape(n, d)
    mem_f = mem.reshape(batch * n_mem, d)

    for l in range(depth):
        wl = w_in[l]
        kr_w = wl[:, o_kr:o_sb]
        w1 = jnp.concatenate([
            wl[:, o_cq:o_kr],
            jnp.pad(kr_w, ((0, 0), (MLA_NOPE, LANES - MLA_NOPE - MLA_ROPE))),
            jnp.pad(_swap_halves(kr_w), ((0, 0), (MLA_NOPE, LANES - MLA_NOPE - MLA_ROPE))),
        ], axis=1).astype(BF16)
        uq = w_uq[l].reshape(MLA_Q_RANK, MLA_HEADS, MLA_NOPE + MLA_ROPE)
        uq_nope, uq_rope = uq[..., :MLA_NOPE], uq[..., MLA_NOPE:]
        padq = ((0, 0), (0, 0), (0, LANES - MLA_NOPE - MLA_ROPE))
        wq = jnp.pad(jnp.concatenate([uq_nope, uq_rope], -1), padq).reshape(MLA_Q_RANK, -1).astype(BF16)
        wqs = jnp.pad(jnp.concatenate([jnp.zeros_like(uq_nope), _swap_halves(uq_rope)], -1), padq)
        wqs = wqs.reshape(MLA_Q_RANK, -1).astype(BF16)
        ukv = w_ukv[l].reshape(MLA_KV_RANK, MLA_HEADS, MLA_NOPE + MLA_V)
        wk = _pad_heads(ukv[..., :MLA_NOPE].reshape(MLA_KV_RANK, -1), MLA_HEADS, MLA_NOPE).astype(BF16)
        wv = _pad_heads(ukv[..., MLA_NOPE:].reshape(MLA_KV_RANK, -1), MLA_HEADS, MLA_V).astype(BF16)
        q_a, k_a, v_a = mla_prep(xf, attn_norm[l], w1, q_norm[l], kv_norm[l], wq, wqs, wk, wv,
                                 cos_t, sin_t, tm=tm, seq=seq)

        sb = wl[:, o_sb:o_diff]
        dq = wl[:, o_diff:o_diff + diff_qk].reshape(d, DIFF_HEADS, 2, DIFF_DIM)
        dq = jnp.stack([jnp.pad(dq[:, :, 0], ((0, 0), (0, 0), (0, DIFF_DIM))),
                        jnp.pad(dq[:, :, 1], ((0, 0), (0, 0), (DIFF_DIM, 0)))], axis=2)
        w2 = jnp.concatenate([
            _pad_heads(sb[:, :sb_w], SB_HEADS, SB_DIM),
            _pad_heads(sb[:, sb_w:2 * sb_w], SB_HEADS, SB_DIM),
            _pad_heads(sb[:, 2 * sb_w:], SB_HEADS, SB_DIM),
            dq.reshape(d, 2 * DIFF_HEADS * LANES),
            wl[:, o_diff + diff_qk:o_diff + 3 * diff_qk],
        ], axis=1).astype(BF16)
        p2 = norm_matmul(xf, attn_norm[l], w2, tm=tm, tn=w2.shape[1] // 2)
        cb_sb_q, cb_sb_k, cb_sb_v = 0, SB_HEADS, 2 * SB_HEADS
        cb_d_q = 3 * SB_HEADS
        cb_d_k = cb_d_q + 2 * DIFF_HEADS
        cb_d_v = cb_d_k + DIFF_HEADS

        o_a = flash_attention(q_a, k_a, v_a, batch=batch, seq=seq, heads=MLA_HEADS, q_col0=0, k_col0=0,
                              v_col0=0, heads_per_kv=1, tq=tq_attn, tk=tk_attn, ts=ts_attn)
        o_b = sb_attention(p2, p2, p2, batch=batch, seq=seq, heads=SB_HEADS, q_col0=cb_sb_q,
                           k_col0=cb_sb_k, v_col0=cb_sb_v, tq=tq_attn, tk=tk_attn, tw=min(256, tk_attn), ts=ts_sb,
                           q_scale=SB_DIM ** -0.5 * LOG2E)
        slopes = tuple(2.0 ** (-8.0 * (i + 1) / DIFF_HEADS) for i in range(DIFF_HEADS))
        o_c = flash_attention(p2, p2, p2, batch=batch, seq=seq, heads=2 * DIFF_HEADS, q_col0=cb_d_q,
                              k_col0=cb_d_k, v_col0=cb_d_v, heads_per_kv=2, tq=tq_attn, tk=tk_attn, ts=ts_attn,
                              q_scale=DIFF_DIM ** -0.5 * LOG2E, slopes=slopes, heads_per_slope=2, out_dtype=F32)

        lam_init = 0.8 - 0.6 * math.exp(-0.3 * l)
        wba = _pad_head_rows(w_branch[l, 0], MLA_HEADS, MLA_V).astype(BF16)
        wbb = _pad_head_rows(w_branch[l, 1], SB_HEADS, SB_DIM).astype(BF16)
        xf = merge(xf, attn_norm[l], wl[:, :gate_cols].astype(BF16), o_a, o_b, o_c, diff_lambda[l],
                   diff_subln[l], wba, wbb, w_branch[l, 2].astype(BF16), w_out[l].astype(BF16),
                   lam_init=lam_init, tm=min(512, seq))

        kv = norm_matmul(mem_f, mem_norm, w_xkv[l].astype(BF16), tm=min(256, n_mem), tn=512)
        xf = cross_attention(xf, cross_norm[l], w_xq[l].astype(BF16), kv, w_xo[l].astype(BF16),
                             batch=batch, seq=seq, tm=tm)

        pad_r = LANES - N_EXPERTS - N_GROUPS
        wr = jnp.pad(jnp.concatenate([w_router_exp[l], w_router_grp[l]], axis=1), ((0, 0), (0, pad_r)))
        br = jnp.pad(jnp.concatenate([b_router_exp[l], b_router_grp[l]]), (0, pad_r)).reshape(1, LANES)
        comb = router(xf, ffn_norm[l], wr, br, tm=tm)
        xf = moe(xf, ffn_norm[l], comb, w_exp_gate[l].astype(BF16), w_exp_up[l].astype(BF16),
                 w_exp_down[l].astype(BF16), final_norm, tm=tm, ec=EXPERTS_PER_GROUP,
                 final_norm=(l == depth - 1))

    return xf.reshape(batch, seq, d)
```

```python
import functools
import math

import jax
import jax.numpy as jnp
from jax import lax
from jax.experimental import pallas as pl
from jax.experimental.pallas import tpu as pltpu

F32 = jnp.float32
BF16 = jnp.bfloat16
EPS = 1e-6
LANES = 128
LOG2E = math.log2(math.e)
VMEM_LIMIT_BYTES = 56 * 1024 * 1024

ROPE_THETA = 10000.0
MLA_HEADS, MLA_NOPE, MLA_ROPE, MLA_V = 8, 64, 32, 64
MLA_Q_RANK, MLA_KV_RANK = 256, 256
SB_HEADS, SB_DIM = 8, 64
DIFF_HEADS, DIFF_DIM = 4, 64
X_HEADS, X_DIM = 4, 128
N_GROUPS, EXPERTS_PER_GROUP = 4, 8
N_EXPERTS = N_GROUPS * EXPERTS_PER_GROUP
GROUP_LANE0 = N_EXPERTS


def _params(*semantics):
    return pltpu.CompilerParams(dimension_semantics=semantics, vmem_limit_bytes=VMEM_LIMIT_BYTES)


def _rms(x, g):
    return x * lax.rsqrt(jnp.mean(x * x, axis=-1, keepdims=True) + EPS) * g


def _dot(a, b):
    return jnp.dot(a, b, preferred_element_type=F32)


def _dot_nt(a, b):
    return lax.dot_general(a, b, (((1,), (1,)), ((), ())), preferred_element_type=F32)


def _norm_matmul_kernel(x_ref, g_ref, w_ref, o_ref, h_scr):
    @pl.when(pl.program_id(1) == 0)
    def _():
        h_scr[...] = _rms(x_ref[...], g_ref[...]).astype(BF16)

    o_ref[...] = _dot(h_scr[...], w_ref[...]).astype(o_ref.dtype)


def norm_matmul(x, g, w, *, tm, tn, out_dtype=BF16):
    n, k = x.shape
    m = w.shape[1]
    return pl.pallas_call(
        _norm_matmul_kernel,
        out_shape=jax.ShapeDtypeStruct((n, m), out_dtype),
        grid=(n // tm, m // tn),
        in_specs=[
            pl.BlockSpec((tm, k), lambda i, j: (i, 0)),
            pl.BlockSpec((1, k), lambda i, j: (0, 0)),
            pl.BlockSpec((k, tn), lambda i, j: (0, j)),
        ],
        out_specs=pl.BlockSpec((tm, tn), lambda i, j: (i, j)),
        scratch_shapes=[pltpu.VMEM((tm, k), BF16)],
        compiler_params=_params("parallel", "arbitrary"),
        name="norm_matmul",
    )(x, g.reshape(1, k), w)


def _mla_prep_kernel(x_ref, g_ref, w1_ref, qn_ref, kvn_ref, wq_ref, wqs_ref, wk_ref, wv_ref,
                     cos_ref, sin_ref, q_ref, k_ref, v_ref, *, scale):
    h = _rms(x_ref[...], g_ref[...]).astype(BF16)
    c = _dot(h, w1_ref[...])
    cq = _rms(c[:, :MLA_Q_RANK], qn_ref[...]).astype(BF16)
    ckv = _rms(c[:, MLA_Q_RANK:MLA_Q_RANK + MLA_KV_RANK], kvn_ref[...]).astype(BF16)
    base = MLA_Q_RANK + MLA_KV_RANK
    cos = cos_ref[...]
    sin = sin_ref[...]
    kr = c[:, base:base + LANES] * cos + c[:, base + LANES:base + 2 * LANES] * sin
    q = _dot(cq, wq_ref[...])
    qs = _dot(cq, wqs_ref[...])
    k = _dot(ckv, wk_ref[...])
    v_ref[...] = _dot(ckv, wv_ref[...]).astype(BF16)
    for hh in range(MLA_HEADS):
        sl = slice(hh * LANES, (hh + 1) * LANES)
        q_ref[:, sl] = ((q[:, sl] * cos + qs[:, sl] * sin) * scale).astype(BF16)
        k_ref[:, sl] = (k[:, sl] + kr).astype(BF16)


def mla_prep(x, g, w1, qn, kvn, wq, wqs, wk, wv, cos_t, sin_t, *, tm, seq):
    n, d = x.shape
    width = MLA_HEADS * LANES
    steps_per_seq = seq // tm
    full = lambda a: pl.BlockSpec(a.shape, lambda i: (0,) * a.ndim)
    out = jax.ShapeDtypeStruct((n, width), BF16)
    row = pl.BlockSpec((tm, width), lambda i: (i, 0))
    args = (g.reshape(1, d), w1, qn.reshape(1, -1), kvn.reshape(1, -1), wq, wqs, wk, wv)
    return pl.pallas_call(
        functools.partial(_mla_prep_kernel, scale=(MLA_NOPE + MLA_ROPE) ** -0.5 * LOG2E),
        out_shape=(out, out, out),
        grid=(n // tm,),
        in_specs=[pl.BlockSpec((tm, d), lambda i: (i, 0))] + [full(a) for a in args] + [
            pl.BlockSpec((tm, LANES), lambda i: (i % steps_per_seq, 0)),
            pl.BlockSpec((tm, LANES), lambda i: (i % steps_per_seq, 0)),
        ],
        out_specs=(row, row, row),
        compiler_params=_params("parallel"),
        name="mla_prep",
    )(x, *args, cos_t, sin_t)


def _flash_kernel(q_ref, k_ref, v_ref, o_ref, s_scr, p_scr, alpha_scr, m_scr, l_scr, acc_scr, *, tq, tk, ts,
                  q_scale, slopes, heads_per_slope):
    qi = pl.program_id(2)
    nchunk = tk // LANES
    n_full = 2 * qi
    m_scr[...] = jnp.full(m_scr.shape, -jnp.inf, F32)
    l_scr[...] = jnp.zeros(l_scr.shape, F32)
    acc_scr[...] = jnp.zeros(acc_scr.shape, F32)
    q = q_ref[...]
    if q_scale != 1.0:
        q = (q.astype(F32) * q_scale).astype(BF16)

    if slopes is not None:
        slope_idx = pl.program_id(1) // heads_per_slope
        slope = jnp.float32(slopes[-1])
        for i in range(len(slopes) - 2, -1, -1):
            slope = jnp.where(slope_idx == i, jnp.float32(slopes[i]), slope)
        col = lax.broadcasted_iota(jnp.int32, (1, tk), 1)
    lane_minus_row = (lax.broadcasted_iota(jnp.int32, (ts, LANES), 1)
                      - lax.broadcasted_iota(jnp.int32, (ts, LANES), 0))

    def scores(kb, slot):
        s = _dot_nt(q, k_ref[pl.ds(pl.multiple_of(kb * tk, tk), tk), :])
        if slopes is not None:
            s = s + (slope * LOG2E) * (col + (kb * tk - qi * tq)).astype(F32)
        s_scr[slot] = s

    def weighted_values(kb, slot):
        pv = _dot(p_scr[slot], v_ref[pl.ds(pl.multiple_of(kb * tk, tk), tk), :])
        acc_scr[...] = alpha_scr[slot] * acc_scr[...] + pv

    def softmax(slot, local):
        for u in range(tq // ts):
            rows = slice(u * ts, (u + 1) * ts)
            last = u * ts + ts - 1
            live = [c for c in range(nchunk) if local is None or local + c * LANES <= last]
            if not live:
                p_scr[slot, rows, :] = jnp.zeros((ts, tk), BF16)
                alpha_scr[slot, rows, :] = jnp.ones((ts, LANES), F32)
                continue
            sc = [s_scr[slot, rows, c * LANES:(c + 1) * LANES] for c in live]
            if local is not None:
                for i, c in enumerate(live):
                    if local + c * LANES + LANES - 1 > u * ts:
                        sc[i] = jnp.where(lane_minus_row <= u * ts - local - c * LANES, sc[i], -jnp.inf)
            m_prev = m_scr[rows, :]
            m_new = jnp.maximum(m_prev, jnp.max(functools.reduce(jnp.maximum, sc), axis=1, keepdims=True))
            alpha = jnp.exp2(m_prev - m_new)
            ps = [jnp.exp2(x - m_new) for x in sc]
            l_scr[rows, :] = alpha * l_scr[rows, :] + jnp.sum(functools.reduce(jnp.add, ps), axis=1, keepdims=True)
            for c, pc in zip(live, ps):
                p_scr[slot, rows, c * LANES:(c + 1) * LANES] = pc.astype(BF16)
            if len(live) < nchunk:
                p_scr[slot, rows, len(live) * LANES:] = jnp.zeros((ts, tk - len(live) * LANES), BF16)
            alpha_scr[slot, rows, :] = alpha
            m_scr[rows, :] = m_new

    last_full = jnp.maximum(n_full - 1, 0)
    scores(n_full, 0)
    scores(n_full + 1, 1)
    softmax(0, 0)
    weighted_values(n_full, 0)
    softmax(1, tk)
    scores(0, 2)

    def trip(j, slot, last=False):
        weighted_values(jnp.where(j == 0, n_full + 1, j - 1), (slot - 1) % 4)
        softmax(slot, None)
        if not last:
            scores(jnp.minimum(j + 1, last_full), (slot + 1) % 4)

    def quad(i, carry):
        trip(4 * i, 2)
        trip(4 * i + 1, 3)
        trip(4 * i + 2, 0)
        trip(4 * i + 3, 1)
        return carry

    lax.fori_loop(0, qi // 2, quad, 0)

    @pl.when(qi % 2 == 1)
    def _():
        trip(n_full - 2, 2)
        trip(n_full - 1, 3, last=True)
        weighted_values(n_full - 1, 3)

    @pl.when(qi % 2 == 0)
    def _():
        weighted_values(jnp.where(qi == 0, 1, n_full - 1), 1)

    o_ref[...] = (acc_scr[...] / l_scr[...]).astype(o_ref.dtype)


def flash_attention(q, k, v, *, batch, seq, heads, q_col0, k_col0, v_col0, heads_per_kv, tq, tk, ts,
                    q_scale=1.0, slopes=None, heads_per_slope=1, out_dtype=BF16):
    nq = seq // tq
    assert tq == 2 * tk
    kernel = functools.partial(_flash_kernel, tq=tq, tk=tk, ts=ts, q_scale=q_scale, slopes=slopes,
                               heads_per_slope=heads_per_slope)
    return pl.pallas_call(
        kernel,
        out_shape=jax.ShapeDtypeStruct((batch * seq, heads * LANES), out_dtype),
        grid=(batch, heads, nq),
        in_specs=[
            pl.BlockSpec((tq, LANES), lambda b, h, i: (b * nq + i, q_col0 + h)),
            pl.BlockSpec((seq, LANES), lambda b, h, i: (b, k_col0 + h // heads_per_kv)),
            pl.BlockSpec((seq, LANES), lambda b, h, i: (b, v_col0 + h // heads_per_kv)),
        ],
        out_specs=pl.BlockSpec((tq, LANES), lambda b, h, i: (b * nq + i, h)),
        scratch_shapes=[pltpu.VMEM((4, tq, tk), F32), pltpu.VMEM((4, tq, tk), BF16),
                        pltpu.VMEM((4, tq, LANES), F32)] + [pltpu.VMEM((tq, LANES), F32)] * 3,
        compiler_params=_params("parallel", "parallel", "arbitrary"),
        name="flash_attention",
    )(q, k, v)


def _sb_kernel(q_ref, k_ref, v_ref, tri_ref, o_ref, z_scr, p_scr, c_scr, acc_scr, *, tq, tk, tw, ts, q_scale):
    qi = pl.program_id(2)
    nchunk = tk // LANES
    q = (q_ref[...].astype(F32) * q_scale).astype(BF16)
    tri = tri_ref[...]
    c_scr[...] = jnp.zeros(c_scr.shape, F32)
    acc_scr[...] = jnp.zeros(acc_scr.shape, F32)
    lane_minus_row = (lax.broadcasted_iota(jnp.int32, (ts, LANES), 1)
                      - lax.broadcasted_iota(jnp.int32, (ts, LANES), 0))

    def scores(kc, slot):
        z_scr[slot] = _dot_nt(q, k_ref[pl.ds(pl.multiple_of(kc * tk, tk), tk), :])

    def weighted_values(kc, slot):
        acc_scr[...] += _dot(p_scr[slot], v_ref[pl.ds(pl.multiple_of(kc * tk, tk), tk), :])

    def weights(slot, local):
        for u in range(tq // ts):
            rows = slice(u * ts, (u + 1) * ts)
            def chunk_state(c):
                if local is None or local + c * LANES + LANES - 1 < u * ts:
                    return "full"
                return "dead" if local + c * LANES >= u * ts + ts - 1 else "masked"

            state = [chunk_state(c) for c in range(nchunk)]
            lss, lks, valid = {}, {}, {}
            for c in range(nchunk):
                if state[c] == "dead":
                    continue
                z = z_scr[slot, rows, c * LANES:(c + 1) * LANES]
                lss[c] = jnp.minimum(z, 0.0) - jnp.log2(1.0 + jnp.exp2(-jnp.abs(z)))
                lks[c] = lss[c] - z
                if state[c] == "masked":
                    valid[c] = lane_minus_row < u * ts - local - c * LANES
                    lks[c] = jnp.where(valid[c], lks[c], 0.0)
            later = c_scr[rows, :]
            gc = tw // LANES
            for g in range(nchunk // gc - 1, -1, -1):
                chunks = range(g * gc, (g + 1) * gc)
                live = [c for c in chunks if state[c] != "dead"]
                for c in chunks:
                    if state[c] == "dead":
                        p_scr[slot, rows, c * LANES:(c + 1) * LANES] = jnp.zeros((ts, LANES), BF16)
                if not live:
                    continue
                grp = [lks[c] if c in lks else jnp.zeros((ts, LANES), F32) for c in chunks]
                after = jnp.concatenate([later] * gc, axis=1) + _dot(jnp.concatenate(grp, axis=1).astype(BF16), tri)
                for c in live:
                    i = c - g * gc
                    p = jnp.exp2(lss[c] + after[:, i * LANES:(i + 1) * LANES])
                    if state[c] == "masked":
                        p = jnp.where(valid[c], p, 0.0)
                    p_scr[slot, rows, c * LANES:(c + 1) * LANES] = p.astype(BF16)
                later = later + jnp.sum(functools.reduce(jnp.add, [lks[c] for c in live]), axis=1, keepdims=True)
            if lks:
                c_scr[rows, :] = later

    n = 2 * (qi + 1)
    scores(n - 1, 0)
    weights(0, tk)
    scores(n - 2, 1)
    weighted_values(n - 1, 0)
    weights(1, 0)
    scores(jnp.maximum(n - 3, 0), 2)

    def trip(t, slot, last=False):
        kc = n - 1 - t
        weighted_values(kc + 1, (slot - 1) % 4)
        weights(slot, None)
        if not last:
            scores(jnp.maximum(kc - 1, 0), (slot + 1) % 4)

    def quad(i, carry):
        t = 2 + 4 * i
        trip(t, 2)
        trip(t + 1, 3)
        trip(t + 2, 0)
        trip(t + 3, 1)
        return carry

    lax.fori_loop(0, qi // 2, quad, 0)

    @pl.when(qi % 2 == 1)
    def _():
        t = 2 + 4 * (qi // 2)
        trip(t, 2)
        trip(t + 1, 3, last=True)
        weighted_values(0, 3)

    @pl.when(qi % 2 == 0)
    def _():
        weighted_values(0, 1)

    o_ref[...] = acc_scr[...].astype(o_ref.dtype)


def sb_attention(q, k, v, *, batch, seq, heads, q_col0, k_col0, v_col0, tq, tk, tw, ts, q_scale):
    assert tq == 2 * tk and tk % tw == 0
    nq = seq // tq
    r = lax.broadcasted_iota(jnp.int32, (tw, tw), 0)
    c = lax.broadcasted_iota(jnp.int32, (tw, tw), 1)
    tri = (r > c).astype(BF16)
    return pl.pallas_call(
        functools.partial(_sb_kernel, tq=tq, tk=tk, tw=tw, ts=ts, q_scale=q_scale),
        out_shape=jax.ShapeDtypeStruct((batch * seq, heads * LANES), BF16),
        grid=(batch, heads, nq),
        in_specs=[
            pl.BlockSpec((tq, LANES), lambda b, h, i: (b * nq + i, q_col0 + h)),
            pl.BlockSpec((seq, LANES), lambda b, h, i: (b, k_col0 + h)),
            pl.BlockSpec((seq, LANES), lambda b, h, i: (b, v_col0 + h)),
            pl.BlockSpec((tw, tw), lambda b, h, i: (0, 0)),
        ],
        out_specs=pl.BlockSpec((tq, LANES), lambda b, h, i: (b * nq + i, h)),
        scratch_shapes=[pltpu.VMEM((4, tq, tk), F32), pltpu.VMEM((4, tq, tk), BF16),
                        pltpu.VMEM((tq, LANES), F32), pltpu.VMEM((tq, LANES), F32)],
        compiler_params=_params("parallel", "parallel", "arbitrary"),
        name="sb_attention",
    )(q, k, v, tri)


def _merge_kernel(x_ref, g_ref, wg_ref, oa_ref, ob_ref, oc_ref, lam_ref, subln_ref,
                  wba_ref, wbb_ref, wbc_ref, wo_ref, o_ref, *, lam_init):
    x = x_ref[...]
    d = x.shape[1]
    h = _rms(x, g_ref[...]).astype(BF16)
    lp = lam_ref[...]
    lam = (jnp.exp(jnp.sum(lp[0:1] * lp[1:2], axis=1, keepdims=True))
           - jnp.exp(jnp.sum(lp[2:3] * lp[3:4], axis=1, keepdims=True)) + lam_init)
    oc = oc_ref[...]
    parts = []
    for hh in range(DIFF_HEADS):
        p1 = oc[:, (2 * hh) * LANES:(2 * hh + 1) * LANES]
        p2 = oc[:, (2 * hh + 1) * LANES:(2 * hh + 2) * LANES]
        parts.append((_rms(p1 - lam * p2, subln_ref[...]) * (1.0 - lam_init)).astype(BF16))
    ocn = jnp.concatenate(parts, axis=1)

    def gate(i):
        return jax.nn.sigmoid(_dot(h, wg_ref[:, i * d:(i + 1) * d]))

    merged = gate(0) * _dot(oa_ref[...], wba_ref[...])
    merged += gate(1) * _dot(ob_ref[...], wbb_ref[...])
    merged += gate(2) * _dot(ocn, wbc_ref[...])
    o_ref[...] = x + _dot(merged.astype(BF16), wo_ref[...])


def merge(x, g, wg, oa, ob, oc, lam_p, subln, wba, wbb, wbc, wo, *, lam_init, tm):
    n, d = x.shape
    full = lambda a: pl.BlockSpec(a.shape, lambda i: (0,) * a.ndim)
    row = lambda a: pl.BlockSpec((tm, a.shape[1]), lambda i: (i, 0))
    g2 = g.reshape(1, d)
    subln2 = subln.reshape(1, -1)
    return pl.pallas_call(
        functools.partial(_merge_kernel, lam_init=lam_init),
        out_shape=jax.ShapeDtypeStruct((n, d), F32),
        grid=(n // tm,),
        in_specs=[row(x), full(g2), full(wg), row(oa), row(ob), row(oc), full(lam_p), full(subln2),
                  full(wba), full(wbb), full(wbc), full(wo)],
        out_specs=row(x),
        compiler_params=_params("parallel"),
        name="merge",
    )(x, g2, wg, oa, ob, oc, lam_p, subln2, wba, wbb, wbc, wo)


def _cross_kernel(x_ref, g_ref, wq_ref, k_ref, v_ref, wo_ref, o_ref, *, scale):
    x = x_ref[...]
    h = _rms(x, g_ref[...]).astype(BF16)
    q = (_dot(h, wq_ref[...]) * scale).astype(BF16)
    outs = []
    for hh in range(X_HEADS):
        sl = slice(hh * X_DIM, (hh + 1) * X_DIM)
        s = _dot_nt(q[:, sl], k_ref[:, sl])
        p = jnp.exp(s - jnp.max(s, axis=1, keepdims=True))
        o = _dot(p.astype(BF16), v_ref[:, sl]) / jnp.sum(p, axis=1, keepdims=True)
        outs.append(o.astype(BF16))
    o_ref[...] = x + _dot(jnp.concatenate(outs, axis=1), wo_ref[...])


def cross_attention(x, g, wq, kv, wo, *, batch, seq, tm):
    n, d = x.shape
    n_mem = kv.shape[0] // batch
    width = X_HEADS * X_DIM
    steps = seq // tm
    full = lambda a: pl.BlockSpec(a.shape, lambda b, i: (0,) * a.ndim)
    g2 = g.reshape(1, d)
    return pl.pallas_call(
        functools.partial(_cross_kernel, scale=X_DIM ** -0.5),
        out_shape=jax.ShapeDtypeStruct((n, d), F32),
        grid=(batch, steps),
        in_specs=[
            pl.BlockSpec((tm, d), lambda b, i: (b * steps + i, 0)),
            full(g2), full(wq),
            pl.BlockSpec((n_mem, width), lambda b, i: (b, 0)),
            pl.BlockSpec((n_mem, width), lambda b, i: (b, 1)),
            full(wo),
        ],
        out_specs=pl.BlockSpec((tm, d), lambda b, i: (b * steps + i, 0)),
        compiler_params=_params("parallel", "parallel"),
        name="cross_attention",
    )(x, g2, wq, kv, kv, wo)


def _router_kernel(x_ref, g_ref, w_ref, b_ref, comb_ref):
    h = _rms(x_ref[...], g_ref[...])
    logits = jnp.dot(h, w_ref[...], preferred_element_type=F32, precision=lax.Precision.HIGHEST) + b_ref[...]
    lane = lax.broadcasted_iota(jnp.int32, logits.shape, 1).astype(F32)
    neg = -jnp.inf
    far = float(4 * LANES)

    def first_argmax(vals, vmax):
        return jnp.min(jnp.where(vals == vmax, lane, far), axis=1, keepdims=True)

    gl = jnp.where((lane >= GROUP_LANE0) & (lane < GROUP_LANE0 + N_GROUPS), logits, neg)
    gmax = jnp.max(gl, axis=1, keepdims=True)
    lo = (first_argmax(gl, gmax) - GROUP_LANE0) * EXPERTS_PER_GROUP
    p_g = 1.0 / jnp.sum(jnp.exp(gl - gmax), axis=1, keepdims=True)
    el = jnp.where((lane >= lo) & (lane < lo + EXPERTS_PER_GROUP), logits, neg)
    v1 = jnp.max(el, axis=1, keepdims=True)
    i1 = first_argmax(el, v1)
    el2 = jnp.where(lane == i1, neg, el)
    v2 = jnp.max(el2, axis=1, keepdims=True)
    i2 = first_argmax(el2, v2)
    t = jnp.exp(v2 - v1)
    w1 = 1.0 / (1.0 + t)
    comb_ref[...] = jnp.where(lane == i1, w1 * p_g, 0.0) + jnp.where(lane == i2, (t * w1) * p_g, 0.0)


def router(x, g, w, b, *, tm):
    n, d = x.shape
    full = lambda a: pl.BlockSpec(a.shape, lambda i: (0,) * a.ndim)
    g2 = g.reshape(1, d)
    return pl.pallas_call(
        _router_kernel,
        out_shape=jax.ShapeDtypeStruct((n, LANES), F32),
        grid=(n // tm,),
        in_specs=[pl.BlockSpec((tm, d), lambda i: (i, 0)), full(g2), full(w), full(b)],
        out_specs=pl.BlockSpec((tm, LANES), lambda i: (i, 0)),
        compiler_params=_params("parallel"),
        name="router",
    )(x, g2, w, b)


def _moe_kernel(x_ref, g_ref, comb_ref, wg_ref, wu_ref, wd_ref, fg_ref, o_ref, h_scr, y_scr, *, ec, final_norm):
    j = pl.program_id(1)

    @pl.when(j == 0)
    def _():
        h_scr[...] = _rms(x_ref[...], g_ref[...]).astype(BF16)
        y_scr[...] = jnp.zeros(y_scr.shape, F32)

    h = h_scr[...]
    comb = comb_ref[...]
    lane = lax.broadcasted_iota(jnp.int32, comb.shape, 1)
    for e in range(ec):
        a = _dot(h, wg_ref[e])
        u = _dot(h, wu_ref[e])
        cw = jnp.sum(jnp.where(lane == j * ec + e, comb, 0.0), axis=1, keepdims=True)
        act = (a * jax.nn.sigmoid(a)) * u * cw
        y_scr[...] += _dot(act.astype(BF16), wd_ref[e])

    @pl.when(j == pl.num_programs(1) - 1)
    def _():
        y = x_ref[...] + y_scr[...]
        if final_norm:
            y = _rms(y, fg_ref[...])
        o_ref[...] = y


def moe(x, g, comb, wg, wu, wd, fg, *, tm, ec, final_norm):
    n, d = x.shape
    n_exp, _, f = wg.shape
    g2 = g.reshape(1, d)
    fg2 = fg.reshape(1, d)
    return pl.pallas_call(
        functools.partial(_moe_kernel, ec=ec, final_norm=final_norm),
        out_shape=jax.ShapeDtypeStruct((n, d), F32),
        grid=(n // tm, n_exp // ec),
        in_specs=[
            pl.BlockSpec((tm, d), lambda i, j: (i, 0)),
            pl.BlockSpec((1, d), lambda i, j: (0, 0)),
            pl.BlockSpec((tm, LANES), lambda i, j: (i, 0)),
            pl.BlockSpec((ec, d, f), lambda i, j: (j, 0, 0)),
            pl.BlockSpec((ec, d, f), lambda i, j: (j, 0, 0)),
            pl.BlockSpec((ec, f, d), lambda i, j: (j, 0, 0)),
            pl.BlockSpec((1, d), lambda i, j: (0, 0)),
        ],
        out_specs=pl.BlockSpec((tm, d), lambda i, j: (i, 0)),
        scratch_shapes=[pltpu.VMEM((tm, d), BF16), pltpu.VMEM((tm, d), F32)],
        compiler_params=pltpu.CompilerParams(
            dimension_semantics=("parallel", "arbitrary"), vmem_limit_bytes=VMEM_LIMIT_BYTES,
            allow_input_fusion=[False, False, False, True, True, True, False]),
        name="moe",
    )(x, g2, comb, wg, wu, wd, fg2)


def _pad_heads(w, heads, dim, lane0=0):
    k = w.shape[0]
    w = w.reshape(k, heads, dim)
    w = jnp.pad(w, ((0, 0), (0, 0), (lane0, LANES - dim - lane0)))
    return w.reshape(k, heads * LANES)


def _pad_head_rows(w, heads, dim):
    m = w.shape[1]
    w = w.reshape(heads, dim, m)
    w = jnp.pad(w, ((0, 0), (0, LANES - dim), (0, 0)))
    return w.reshape(heads * LANES, m)


def _swap_halves(w):
    half = w.shape[-1] // 2
    return jnp.concatenate([w[..., half:], w[..., :half]], axis=-1)


def _rope_tables(seq):
    half = MLA_ROPE // 2
    inv_freq = ROPE_THETA ** (-jnp.arange(half, dtype=F32) / half)
    ang = jnp.arange(seq, dtype=F32)[:, None] * inv_freq[None, :]
    cos, sin = jnp.cos(ang), jnp.sin(ang)
    ones = jnp.ones((seq, MLA_NOPE), F32)
    zn = jnp.zeros((seq, MLA_NOPE), F32)
    zp = jnp.zeros((seq, LANES - MLA_NOPE - MLA_ROPE), F32)
    return (jnp.concatenate([ones, cos, cos, zp], axis=1),
            jnp.concatenate([zn, -sin, sin, zp], axis=1))


def kernel(x, mem, attn_norm, w_in, q_norm, w_uq, kv_norm, w_ukv, diff_lambda, diff_subln, w_branch, w_out,
           cross_norm, mem_norm, w_xq, w_xkv, w_xo, ffn_norm, w_router_grp, b_router_grp, w_router_exp,
           b_router_exp, w_exp_gate, w_exp_up, w_exp_down, final_norm):
    batch, seq, d = x.shape
    depth = w_in.shape[0]
    n = batch * seq
    n_mem = mem.shape[1]
    tm = min(1024, seq)
    tq_attn = min(1024, seq)
    tk_attn = tq_attn // 2
    ts_attn = min(64, seq)
    ts_sb = min(256, seq)

    gate_cols = 3 * d
    o_cq = gate_cols
    o_ckv = o_cq + MLA_Q_RANK
    o_kr = o_ckv + MLA_KV_RANK
    o_sb = o_kr + MLA_ROPE
    sb_w = SB_HEADS * SB_DIM
    o_diff = o_sb + 3 * sb_w
    diff_qk = DIFF_HEADS * 2 * DIFF_DIM

    cos_t, sin_t = _rope_tables(seq)
    xf = x.reshape(n, d)
    mem_f = mem.reshape(batch * n_mem, d)

    for l in range(depth):
        wl = w_in[l]
        kr_w = wl[:, o_kr:o_sb]
        w1 = jnp.concatenate([
            wl[:, o_cq:o_kr],
            jnp.pad(kr_w, ((0, 0), (MLA_NOPE, LANES - MLA_NOPE - MLA_ROPE))),
            jnp.pad(_swap_halves(kr_w), ((0, 0), (MLA_NOPE, LANES - MLA_NOPE - MLA_ROPE))),
        ], axis=1).astype(BF16)
        uq = w_uq[l].reshape(MLA_Q_RANK, MLA_HEADS, MLA_NOPE + MLA_ROPE)
        uq_nope, uq_rope = uq[..., :MLA_NOPE], uq[..., MLA_NOPE:]
        padq = ((0, 0), (0, 0), (0, LANES - MLA_NOPE - MLA_ROPE))
        wq = jnp.pad(jnp.concatenate([uq_nope, uq_rope], -1), padq).reshape(MLA_Q_RANK, -1).astype(BF16)
        wqs = jnp.pad(jnp.concatenate([jnp.zeros_like(uq_nope), _swap_halves(uq_rope)], -1), padq)
        wqs = wqs.reshape(MLA_Q_RANK, -1).astype(BF16)
        ukv = w_ukv[l].reshape(MLA_KV_RANK, MLA_HEADS, MLA_NOPE + MLA_V)
        wk = _pad_heads(ukv[..., :MLA_NOPE].reshape(MLA_KV_RANK, -1), MLA_HEADS, MLA_NOPE).astype(BF16)
        wv = _pad_heads(ukv[..., MLA_NOPE:].reshape(MLA_KV_RANK, -1), MLA_HEADS, MLA_V).astype(BF16)
        q_a, k_a, v_a = mla_prep(xf, attn_norm[l], w1, q_norm[l], kv_norm[l], wq, wqs, wk, wv,
                                 cos_t, sin_t, tm=tm, seq=seq)

        sb = wl[:, o_sb:o_diff]
        dq = wl[:, o_diff:o_diff + diff_qk].reshape(d, DIFF_HEADS, 2, DIFF_DIM)
        dq = jnp.stack([jnp.pad(dq[:, :, 0], ((0, 0), (0, 0), (0, DIFF_DIM))),
                        jnp.pad(dq[:, :, 1], ((0, 0), (0, 0), (DIFF_DIM, 0)))], axis=2)
        w2 = jnp.concatenate([
            _pad_heads(sb[:, :sb_w], SB_HEADS, SB_DIM),
            _pad_heads(sb[:, sb_w:2 * sb_w], SB_HEADS, SB_DIM),
            _pad_heads(sb[:, 2 * sb_w:], SB_HEADS, SB_DIM),
            dq.reshape(d, 2 * DIFF_HEADS * LANES),
            wl[:, o_diff + diff_qk:o_diff + 3 * diff_qk],
        ], axis=1).astype(BF16)
        p2 = norm_matmul(xf, attn_norm[l], w2, tm=tm, tn=w2.shape[1] // 2)
        cb_sb_q, cb_sb_k, cb_sb_v = 0, SB_HEADS, 2 * SB_HEADS
        cb_d_q = 3 * SB_HEADS
        cb_d_k = cb_d_q + 2 * DIFF_HEADS
        cb_d_v = cb_d_k + DIFF_HEADS

        o_a = flash_attention(q_a, k_a, v_a, batch=batch, seq=seq, heads=MLA_HEADS, q_col0=0, k_col0=0,
                              v_col0=0, heads_per_kv=1, tq=tq_attn, tk=tk_attn, ts=ts_attn)
        o_b = sb_attention(p2, p2, p2, batch=batch, seq=seq, heads=SB_HEADS, q_col0=cb_sb_q,
                           k_col0=cb_sb_k, v_col0=cb_sb_v, tq=tq_attn, tk=tk_attn, tw=min(256, tk_attn), ts=ts_sb,
                           q_scale=SB_DIM ** -0.5 * LOG2E)
        slopes = tuple(2.0 ** (-8.0 * (i + 1) / DIFF_HEADS) for i in range(DIFF_HEADS))
        o_c = flash_attention(p2, p2, p2, batch=batch, seq=seq, heads=2 * DIFF_HEADS, q_col0=cb_d_q,
                              k_col0=cb_d_k, v_col0=cb_d_v, heads_per_kv=2, tq=tq_attn, tk=tk_attn, ts=ts_attn,
                              q_scale=DIFF_DIM ** -0.5 * LOG2E, slopes=slopes, heads_per_slope=2, out_dtype=F32)

        lam_init = 0.8 - 0.6 * math.exp(-0.3 * l)
        wba = _pad_head_rows(w_branch[l, 0], MLA_HEADS, MLA_V).astype(BF16)
        wbb = _pad_head_rows(w_branch[l, 1], SB_HEADS, SB_DIM).astype(BF16)
        xf = merge(xf, attn_norm[l], wl[:, :gate_cols].astype(BF16), o_a, o_b, o_c, diff_lambda[l],
                   diff_subln[l], wba, wbb, w_branch[l, 2].astype(BF16), w_out[l].astype(BF16),
                   lam_init=lam_init, tm=min(512, seq))

        kv = norm_matmul(mem_f, mem_norm, w_xkv[l].astype(BF16), tm=min(256, n_mem), tn=512)
        xf = cross_attention(xf, cross_norm[l], w_xq[l].astype(BF16), kv, w_xo[l].astype(BF16),
                             batch=batch, seq=seq, tm=tm)

        pad_r = LANES - N_EXPERTS - N_GROUPS
        wr = jnp.pad(jnp.concatenate([w_router_exp[l], w_router_grp[l]], axis=1), ((0, 0), (0, pad_r)))
        br = jnp.pad(jnp.concatenate([b_router_exp[l], b_router_grp[l]]), (0, pad_r)).reshape(1, LANES)
        comb = router(xf, ffn_norm[l], wr, br, tm=tm)
        xf = moe(xf, ffn_norm[l], comb, w_exp_gate[l].astype(BF16), w_exp_up[l].astype(BF16),
                 w_exp_down[l].astype(BF16), final_norm, tm=tm, ec=EXPERTS_PER_GROUP,
                 final_norm=(l == depth - 1))

    return xf.reshape(batch, seq, d)
```
